```python
import math
import jax, jax.numpy as jnp
from jax import lax
import numpy as np

D_MODEL = 1024
BATCH = 4
SEQ = 8192
DEPTH = 4

N_MIXERS = 2
N_ATTN_LAYERS = (DEPTH + 1) // 2
N_SSM_LAYERS = DEPTH // 2

D_FF = 2816

ATTN_HEADS = 16
ATTN_HEAD_DIM = 64
KV_RANK = 256
IDX_HEADS = 8
IDX_HEAD_DIM = 64
TOPK_MAX = 256
Q_BLOCK = 128
ATTN_Q_DIM = ATTN_HEADS * ATTN_HEAD_DIM
ATTN_IN = ATTN_Q_DIM + KV_RANK + IDX_HEADS * IDX_HEAD_DIM + IDX_HEAD_DIM + IDX_HEADS

SSM_INNER = 2 * D_MODEL
SSM_HEAD_DIM = 64
SSM_HEADS = SSM_INNER // SSM_HEAD_DIM
SSM_GROUPS = 4
SSM_STATE = 128
SSM_CONV = 4
SSM_CHUNK = 128
SSM_CONV_DIM = SSM_INNER + 2 * SSM_GROUPS * SSM_STATE
SSM_IN = SSM_INNER + SSM_CONV_DIM + SSM_HEADS

DEEPNORM_ALPHA = (2.0 * DEPTH) ** 0.25
DEEPNORM_BETA = (8.0 * DEPTH) ** -0.25
LN_EPS = 1e-5
RMS_EPS = 1e-6

kernel_name = 'hybrid_dsa_ssd_macaron_deepnorm'


def layer_norm(x, g, b):
    xf = x.astype(jnp.float32)
    mu = jnp.mean(xf, -1, keepdims=True)
    var = jnp.mean(jnp.square(xf - mu), -1, keepdims=True)
    return ((xf - mu) * lax.rsqrt(var + LN_EPS) * g.astype(jnp.float32) + b.astype(jnp.float32)).astype(x.dtype)


def rms_norm(x, g):
    xf = x.astype(jnp.float32)
    ms = jnp.mean(jnp.square(xf), -1, keepdims=True)
    return (xf * lax.rsqrt(ms + RMS_EPS) * g.astype(jnp.float32)).astype(x.dtype)


def swiglu_ffn(x, w_in, w_out):
    gate, up = jnp.split(x @ w_in, 2, axis=-1)
    return (jax.nn.silu(gate) * up) @ w_out


def dsa_mixer(x, w_in, kv_norm_g, w_uk, w_uv, w_out):
    bsz, seq, _ = x.shape
    topk = min(TOPK_MAX, seq // 4)
    o1 = ATTN_Q_DIM
    o2 = o1 + KV_RANK
    o3 = o2 + IDX_HEADS * IDX_HEAD_DIM
    o4 = o3 + IDX_HEAD_DIM
    q, c_kv, q_idx, k_idx, w_idx = jnp.split(x @ w_in, [o1, o2, o3, o4], axis=-1)
    q = q.reshape(bsz, seq, ATTN_HEADS, ATTN_HEAD_DIM)
    c_kv = rms_norm(c_kv, kv_norm_g)
    q_idx = q_idx.reshape(bsz, seq, IDX_HEADS, IDX_HEAD_DIM)
    w_idx = w_idx * (IDX_HEADS ** -0.5)
    n_blk = seq // Q_BLOCK
    key_pos = jnp.arange(seq)

    def to_blocks(t):
        return jnp.moveaxis(t.reshape(bsz, n_blk, Q_BLOCK, *t.shape[2:]), 1, 0)

    def block(args):
        blk, q_b, qi_b, wi_b = args
        q_pos = blk * Q_BLOCK + jnp.arange(Q_BLOCK)
        s = jnp.einsum('bthd,bsd->bths', qi_b, k_idx).astype(jnp.float32) * (IDX_HEAD_DIM ** -0.5)
        score = jnp.einsum('bths,bth->bts', jax.nn.relu(s), wi_b.astype(jnp.float32))
        causal = key_pos[None, :] <= q_pos[:, None]
        score = jnp.where(causal[None], score, -jnp.inf)
        _, idx = lax.top_k(score, topk)
        valid = idx <= q_pos[None, :, None]
        c_sel = jax.vmap(lambda c, i: c[i])(c_kv, idx)
        q_abs = jnp.einsum('bthd,hdr->bthr', q_b, w_uk)
        logits = jnp.einsum('bthr,btkr->bthk', q_abs, c_sel).astype(jnp.float32) * (ATTN_HEAD_DIM ** -0.5)
        logits = jnp.where(valid[:, :, None, :], logits, -jnp.inf)
        p = jax.nn.softmax(logits, axis=-1).astype(c_sel.dtype)
        o_lat = jnp.einsum('bthk,btkr->bthr', p, c_sel)
        o = jnp.einsum('bthr,hrd->bthd', o_lat, w_uv)
        return o.reshape(bsz, Q_BLOCK, ATTN_Q_DIM)

    out = lax.map(block, (jnp.arange(n_blk), to_blocks(q), to_blocks(q_idx), to_blocks(w_idx)))
    out = jnp.moveaxis(out, 0, 1).reshape(bsz, seq, ATTN_Q_DIM)
    return out @ w_out


def ssd_scan(x, dt, a, b_in, c_in):
    bsz, seq, n_h, p_dim = x.shape
    nc = seq // SSM_CHUNK
    q_len = SSM_CHUNK
    g_n = SSM_GROUPS
    r_n = n_h // g_n
    xdt = (x.astype(jnp.float32) * dt[..., None]).reshape(bsz, nc, q_len, g_n, r_n, p_dim)
    a_dt = (dt * a).reshape(bsz, nc, q_len, g_n, r_n)
    a_cs_l = jnp.cumsum(a_dt, axis=2)
    a_cs = jnp.moveaxis(a_cs_l, 2, -1)
    bm = b_in.astype(jnp.float32).reshape(bsz, nc, q_len, g_n, SSM_STATE)
    cm = c_in.astype(jnp.float32).reshape(bsz, nc, q_len, g_n, SSM_STATE)
    seg = a_cs[..., :, None] - a_cs[..., None, :]
    tri = jnp.tril(jnp.ones((q_len, q_len), dtype=bool))
    decay = jnp.exp(jnp.where(tri, seg, -jnp.inf))
    cb = jnp.einsum('bclgn,bcsgn->bcgls', cm, bm)
    y_diag = jnp.einsum('bcgrls,bcsgrp->bclgrp', cb[:, :, :, None] * decay, xdt)
    decay_to_end = jnp.exp(a_cs_l[:, :, -1:] - a_cs_l)
    states = jnp.einsum('bclgn,bclgrp->bcgrpn', bm, xdt * decay_to_end[..., None])
    chunk_decay = jnp.exp(a_cs_l[:, :, -1])

    def step(h, inp):
        s_c, d_c = inp
        return h * d_c[..., None, None] + s_c, h

    h0 = jnp.zeros((bsz, g_n, r_n, p_dim, SSM_STATE), jnp.float32)
    _, prev = lax.scan(step, h0, (jnp.moveaxis(states, 1, 0), jnp.moveaxis(chunk_decay, 1, 0)))
    prev = jnp.moveaxis(prev, 0, 1)
    y_off = jnp.einsum('bclgn,bcgrpn->bclgrp', cm, prev) * jnp.exp(a_cs_l)[..., None]
    return (y_diag + y_off).reshape(bsz, seq, n_h, p_dim)


def ssd_mixer(x, w_in, conv_w, conv_b, dt_bias, a_log, d_skip, norm_g, w_out):
    bsz, seq, _ = x.shape
    z, xbc, dt = jnp.split(x @ w_in, [SSM_INNER, SSM_INNER + SSM_CONV_DIM], axis=-1)
    xbc_pad = jnp.pad(xbc, ((0, 0), (SSM_CONV - 1, 0), (0, 0)))
    conv = conv_b + xbc_pad[:, 0:seq] * conv_w[0]
    for k in range(1, SSM_CONV):
        conv = conv + xbc_pad[:, k:k + seq] * conv_w[k]
    xbc = jax.nn.silu(conv)
    xs, b_in, c_in = jnp.split(xbc, [SSM_INNER, SSM_INNER + SSM_GROUPS * SSM_STATE], axis=-1)
    xs = xs.reshape(bsz, seq, SSM_HEADS, SSM_HEAD_DIM)
    dt = jax.nn.softplus((dt + dt_bias).astype(jnp.float32))
    a = -jnp.exp(a_log.astype(jnp.float32))
    y = ssd_scan(xs, dt, a,
                 b_in.reshape(bsz, seq, SSM_GROUPS, SSM_STATE),
                 c_in.reshape(bsz, seq, SSM_GROUPS, SSM_STATE))
    y = y + xs.astype(jnp.float32) * d_skip.astype(jnp.float32)[:, None]
    yz = y.reshape(bsz, seq, SSM_INNER) * jax.nn.silu(z.astype(jnp.float32))
    yz = yz.reshape(bsz, seq, SSM_GROUPS, SSM_INNER // SSM_GROUPS)
    yz = yz * lax.rsqrt(jnp.mean(jnp.square(yz), -1, keepdims=True) + RMS_EPS)
    yz = (yz.reshape(bsz, seq, SSM_INNER) * norm_g.astype(jnp.float32)).astype(x.dtype)
    return yz @ w_out


def setup_inputs(seed: int = 0) -> dict:
    key = jax.random.key(seed)
    ks = jax.random.split(key, 20)
    f32 = jnp.float32
    D = D_MODEL
    nrm = lambda k, shape, s: jax.random.normal(k, shape, f32) * s
    x = jax.random.normal(ks[0], (BATCH, SEQ, D), f32)
    ln_g = 1.0 + nrm(ks[1], (DEPTH, 3, D), 0.02)
    ln_b = nrm(ks[2], (DEPTH, 3, D), 0.02)
    ffn1_w_in = nrm(ks[3], (DEPTH, D, 2 * D_FF), D ** -0.5)
    ffn1_w_out = nrm(ks[4], (DEPTH, D_FF, D), D_FF ** -0.5 * DEEPNORM_BETA)
    ffn2_w_in = nrm(ks[5], (DEPTH, D, 2 * D_FF), D ** -0.5)
    ffn2_w_out = nrm(ks[6], (DEPTH, D_FF, D), D_FF ** -0.5 * DEEPNORM_BETA)
    attn_w_in = nrm(ks[7], (N_ATTN_LAYERS, D, ATTN_IN), D ** -0.5)
    attn_kv_norm = 1.0 + nrm(ks[8], (N_ATTN_LAYERS, KV_RANK), 0.02)
    attn_w_uk = nrm(ks[9], (N_ATTN_LAYERS, ATTN_HEADS, ATTN_HEAD_DIM, KV_RANK), KV_RANK ** -0.5)
    attn_w_uv = nrm(ks[10], (N_ATTN_LAYERS, ATTN_HEADS, KV_RANK, ATTN_HEAD_DIM), KV_RANK ** -0.5)
    attn_w_out = nrm(ks[11], (N_ATTN_LAYERS, ATTN_Q_DIM, D), ATTN_Q_DIM ** -0.5 * DEEPNORM_BETA)
    ssm_w_in = nrm(ks[12], (N_SSM_LAYERS, D, SSM_IN), D ** -0.5)
    ssm_conv_w = nrm(ks[13], (N_SSM_LAYERS, SSM_CONV, SSM_CONV_DIM), SSM_CONV ** -0.5)
    ssm_conv_b = nrm(ks[14], (N_SSM_LAYERS, SSM_CONV_DIM), 0.01)
    dt0 = jnp.exp(jax.random.uniform(ks[15], (N_SSM_LAYERS, SSM_HEADS), f32,
                                     minval=math.log(1e-3), maxval=math.log(1e-1)))
    ssm_dt_bias = dt0 + jnp.log(-jnp.expm1(-dt0))
    ssm_a_log = jnp.log(jax.random.uniform(ks[16], (N_SSM_LAYERS, SSM_HEADS), f32, minval=1.0, maxval=16.0))
    ssm_d = 1.0 + nrm(ks[17], (N_SSM_LAYERS, SSM_HEADS), 0.02)
    ssm_norm_g = 1.0 + nrm(ks[18], (N_SSM_LAYERS, SSM_INNER), 0.02)
    ssm_w_out = nrm(ks[19], (N_SSM_LAYERS, SSM_INNER, D), SSM_INNER ** -0.5 * DEEPNORM_BETA)
    return {'x': x, 'ln_g': ln_g, 'ln_b': ln_b,
            'ffn1_w_in': ffn1_w_in, 'ffn1_w_out': ffn1_w_out,
            'ffn2_w_in': ffn2_w_in, 'ffn2_w_out': ffn2_w_out,
            'attn_w_in': attn_w_in, 'attn_kv_norm': attn_kv_norm,
            'attn_w_uk': attn_w_uk, 'attn_w_uv': attn_w_uv, 'attn_w_out': attn_w_out,
            'ssm_w_in': ssm_w_in, 'ssm_conv_w': ssm_conv_w, 'ssm_conv_b': ssm_conv_b,
            'ssm_dt_bias': ssm_dt_bias, 'ssm_a_log': ssm_a_log, 'ssm_d': ssm_d,
            'ssm_norm_g': ssm_norm_g, 'ssm_w_out': ssm_w_out}


def reference(x, ln_g, ln_b, ffn1_w_in, ffn1_w_out, ffn2_w_in, ffn2_w_out,
              attn_w_in, attn_kv_norm, attn_w_uk, attn_w_uv, attn_w_out,
              ssm_w_in, ssm_conv_w, ssm_conv_b, ssm_dt_bias, ssm_a_log, ssm_d,
              ssm_norm_g, ssm_w_out):
    for i in range(DEPTH):
        x = layer_norm(DEEPNORM_ALPHA * x + 0.5 * swiglu_ffn(x, ffn1_w_in[i], ffn1_w_out[i]), ln_g[i, 0], ln_b[i, 0])
        j = i // N_MIXERS
        if i % N_MIXERS == 0:
            m = dsa_mixer(x, attn_w_in[j], attn_kv_norm[j], attn_w_uk[j], attn_w_uv[j], attn_w_out[j])
        else:
            m = ssd_mixer(x, ssm_w_in[j], ssm_conv_w[j], ssm_conv_b[j], ssm_dt_bias[j],
                          ssm_a_log[j], ssm_d[j], ssm_norm_g[j], ssm_w_out[j])
        x = layer_norm(DEEPNORM_ALPHA * x + m, ln_g[i, 1], ln_b[i, 1])
        x = layer_norm(DEEPNORM_ALPHA * x + 0.5 * swiglu_ffn(x, ffn2_w_in[i], ffn2_w_out[i]), ln_g[i, 2], ln_b[i, 2])
    return x
```

```python
import functools
import math

import jax
import jax.numpy as jnp
from jax import lax
from jax.experimental import pallas as pl
from jax.experimental.pallas import tpu as pltpu

F32 = jnp.float32
BF16 = jnp.bfloat16
I32 = jnp.int32

ATTN_HEADS = 16
ATTN_HEAD_DIM = 64
KV_RANK = 256
IDX_HEADS = 8
IDX_HEAD_DIM = 64
TOPK_MAX = 256
SSM_HEAD_DIM = 64
SSM_GROUPS = 4
SSM_STATE = 128
SSM_CONV = 4
LN_EPS = 1e-5
RMS_EPS = 1e-6

V7X_LANES = 128
V7X_SUBLANES = 8
V7X_MXU_DIM = 256
V7X_VMEM_BYTES = 64 * 1024 * 1024

ROW_TILE = 512
Q_TILE = 128
IDX_KEY_TILE = 512
ATTN_KEY_TILE = V7X_MXU_DIM
SSD_CHUNK = 128
MASKED_LOGIT = -1e30
INT_MIN = -(2 ** 31)
NEG_INF_KEY = INT_MIN + 0x007FFFFF


def _vmem_limit(nbytes):
    return int(min(nbytes, V7X_VMEM_BYTES - 8 * 1024 * 1024))


def _resident(shape):
    nd = len(shape)
    return pl.BlockSpec(shape, lambda *_: (0,) * nd, pipeline_mode=pl.Buffered(1))


def _layer_norm(y, g, b):
    mu = jnp.mean(y, axis=-1, keepdims=True)
    d = y - mu
    var = jnp.mean(d * d, axis=-1, keepdims=True)
    return d * lax.rsqrt(var + LN_EPS) * g + b


def _silu(v):
    return v * jax.nn.sigmoid(v)


def _ffn_ln_kernel(x_ref, win_ref, wout_ref, g_ref, b_ref, o_ref, acc_ref, *, n_chunk, alpha):
    x = x_ref[...]
    xb = x.astype(BF16)
    for c in range(n_chunk):
        gate = jnp.dot(xb, win_ref[c], preferred_element_type=F32)
        up = jnp.dot(xb, win_ref[n_chunk + c], preferred_element_type=F32)
        act = (_silu(gate) * up).astype(BF16)
        part = jnp.dot(act, wout_ref[c], preferred_element_type=F32)
        if c == 0:
            acc_ref[...] = part
        else:
            acc_ref[...] += part
    y = alpha * x + 0.5 * acc_ref[...]
    o_ref[...] = _layer_norm(y, g_ref[...], b_ref[...])


def _ffn_ln(x, win, wout, g, b, alpha):
    t, d = x.shape
    n2, _, ck = win.shape
    n_chunk = n2 // 2
    tm = min(ROW_TILE, t)
    est = (win.size + wout.size) * 2 + 5 * tm * d * 4 + 6 * tm * ck * 4 + (8 << 20)
    return pl.pallas_call(
        functools.partial(_ffn_ln_kernel, n_chunk=n_chunk, alpha=alpha),
        grid=(t // tm,),
        in_specs=[pl.BlockSpec((tm, d), lambda i: (i, 0)),
                  _resident(win.shape), _resident(wout.shape),
                  _resident(g.shape), _resident(b.shape)],
        out_specs=pl.BlockSpec((tm, d), lambda i: (i, 0)),
        out_shape=jax.ShapeDtypeStruct((t, d), F32),
        scratch_shapes=[pltpu.VMEM((tm, d), F32)],
        compiler_params=pltpu.CompilerParams(dimension_semantics=("arbitrary",),
                                             vmem_limit_bytes=_vmem_limit(est)),
        name="ffn_ln",
    )(x, win, wout, g, b)


def _proj_ln_kernel(x_ref, a_ref, w_ref, g_ref, b_ref, o_ref, *, alpha):
    m = jnp.dot(a_ref[...], w_ref[...], preferred_element_type=F32)
    o_ref[...] = _layer_norm(alpha * x_ref[...] + m, g_ref[...], b_ref[...])


def _proj_ln(x, a, w, g, b, alpha):
    t, d = x.shape
    k = a.shape[1]
    tm = min(ROW_TILE, t)
    est = w.size * 2 + 6 * tm * d * 4 + 2 * tm * k * 2 + (8 << 20)
    return pl.pallas_call(
        functools.partial(_proj_ln_kernel, alpha=alpha),
        grid=(t // tm,),
        in_specs=[pl.BlockSpec((tm, d), lambda i: (i, 0)),
                  pl.BlockSpec((tm, k), lambda i: (i, 0)),
                  _resident(w.shape), _resident(g.shape), _resident(b.shape)],
        out_specs=pl.BlockSpec((tm, d), lambda i: (i, 0)),
        out_shape=jax.ShapeDtypeStruct((t, d), F32),
        compiler_params=pltpu.CompilerParams(dimension_semantics=("arbitrary",),
                                             vmem_limit_bytes=_vmem_limit(est)),
        name="proj_ln",
    )(x, a, w, g, b)


def _attn_inproj_kernel(x_ref, wq_ref, wc_ref, wqi_ref, wkw_ref, kvg_ref,
                        q_ref, c_ref, qi_ref, ki_ref, wi_ref, *, w_scale):
    xb = x_ref[...].astype(BF16)
    q_ref[...] = jnp.dot(xb, wq_ref[...], preferred_element_type=F32).astype(BF16)
    c = jnp.dot(xb, wc_ref[...], preferred_element_type=F32)
    ms = jnp.mean(c * c, axis=-1, keepdims=True)
    c_ref[...] = (c * lax.rsqrt(ms + RMS_EPS) * kvg_ref[...]).astype(BF16)
    qi_ref[...] = jnp.dot(xb, wqi_ref[...], preferred_element_type=F32).astype(BF16)
    kw = jnp.dot(xb, wkw_ref[...], preferred_element_type=F32)
    ki_ref[...] = kw[:, :IDX_HEAD_DIM].astype(BF16)
    wi_ref[...] = kw[:, IDX_HEAD_DIM:IDX_HEAD_DIM + IDX_HEADS] * (IDX_HEADS ** -0.5) * w_scale


def _attn_inproj(x, wq, wc, wqi, wkw, kvg):
    t, d = x.shape
    tm = min(ROW_TILE, t)
    nq, nc, nqi = wq.shape[1], wc.shape[1], wqi.shape[1]
    est = (wq.size + wc.size + wqi.size + wkw.size) * 2 + 2 * tm * d * 4 + 8 * tm * (nq + nc + nqi) + (8 << 20)
    row = lambda n: pl.BlockSpec((tm, n), lambda i: (i, 0))
    return pl.pallas_call(
        functools.partial(_attn_inproj_kernel, w_scale=IDX_HEAD_DIM ** -0.5),
        grid=(t // tm,),
        in_specs=[row(d), _resident(wq.shape), _resident(wc.shape), _resident(wqi.shape),
                  _resident(wkw.shape), _resident(kvg.shape)],
        out_specs=[row(nq), row(nc), row(nqi), row(IDX_HEAD_DIM), row(IDX_HEADS)],
        out_shape=[jax.ShapeDtypeStruct((t, nq), BF16), jax.ShapeDtypeStruct((t, nc), BF16),
                   jax.ShapeDtypeStruct((t, nqi), BF16), jax.ShapeDtypeStruct((t, IDX_HEAD_DIM), BF16),
                   jax.ShapeDtypeStruct((t, IDX_HEADS), F32)],
        compiler_params=pltpu.CompilerParams(dimension_semantics=("arbitrary",),
                                             vmem_limit_bytes=_vmem_limit(est)),
        name="attn_inproj",
    )(x, wq, wc, wqi, wkw, kvg)


def _dsa_kernel(x_ref, q_ref, qi_ref, wi_ref, ckv_ref, ckvt_ref, kit_ref, wuk_ref, wuv_ref, wo_ref,
                g_ref, b_ref, o_ref,
                key_scr, qabs_scr, qih_scr, acc_scr, m_scr, l_scr, o_scr, thr_scr, lim_scr,
                *, topk, seq, alpha):
    i = pl.program_id(1)
    tq, h_n, dh, r_n = Q_TILE, ATTN_HEADS, ATTN_HEAD_DIM, KV_RANK
    hi_n, di = IDX_HEADS, IDX_HEAD_DIM
    tki, tk = IDX_KEY_TILE, ATTN_KEY_TILE
    q0 = i * tq
    n_idx_blk = q0 // tki + 1
    n_kv_blk = q0 // tk + 1

    for h in range(h_n):
        qa = jnp.dot(q_ref[0, :, h * dh:(h + 1) * dh], wuk_ref[h], preferred_element_type=F32)
        qabs_scr[h * tq:(h + 1) * tq, :] = (qa * (dh ** -0.5)).astype(BF16)
    for h in range(hi_n):
        qih_scr[h * tq:(h + 1) * tq, :] = qi_ref[0, :, h * di:(h + 1) * di]

    wi = wi_ref[0]
    q_pos = q0 + lax.broadcasted_iota(I32, (tq, 1), 0)

    def score_body(kb, carry):
        off = pl.multiple_of(kb * tki, tki)
        s = jnp.dot(qih_scr[...], kit_ref[0, :, pl.ds(off, tki)], preferred_element_type=F32)
        score = jnp.zeros((tq, tki), F32)
        for h in range(hi_n):
            score = score + jnp.maximum(s[h * tq:(h + 1) * tq, :], 0.0) * wi[:, h:h + 1]
        k_pos = off + lax.broadcasted_iota(I32, (tq, tki), 1)
        score = jnp.where(k_pos <= q_pos, score, -jnp.inf)
        bits = pltpu.bitcast(score, I32)
        key_scr[:, pl.ds(off, tki)] = bits ^ ((bits >> 31) & 0x7FFFFFFF)
        return carry

    lax.fori_loop(0, n_idx_blk, score_body, 0)

    def count(pred):
        def body(kb, c):
            off = pl.multiple_of(kb * tki, tki)
            keys = key_scr[:, pl.ds(off, tki)]
            k_pos = off + lax.broadcasted_iota(I32, (tq, tki), 1)
            hit = jnp.where(pred(keys, k_pos), 1.0, 0.0)
            for j in range(tki // V7X_LANES):
                c = c + hit[:, j * V7X_LANES:(j + 1) * V7X_LANES]
            return c
        c = lax.fori_loop(0, n_idx_blk, body, jnp.zeros((tq, V7X_LANES), F32))
        return jnp.sum(c, axis=-1, keepdims=True)

    kf = float(topk)

    thr0 = jnp.where(count(lambda k, p: k >= 0) >= kf, jnp.zeros((tq, 1), I32),
                     jnp.full((tq, 1), INT_MIN, I32))

    def thr_body(j, thr):
        cand = thr + jnp.left_shift(jnp.int32(1), 30 - j)
        return jnp.where(count(lambda k, p: k >= cand) >= kf, cand, thr)

    thr = lax.fori_loop(0, 31, thr_body, thr0)
    thr = jnp.maximum(thr, NEG_INF_KEY + 1)
    n_ge = count(lambda k, p: k >= thr)
    n_gt = count(lambda k, p: k > thr)
    tied = n_ge > kf
    need = kf - n_gt
    thr_scr[...] = thr
    lim_scr[...] = jnp.full((tq, 1), seq, I32)

    @pl.when(jnp.max(jnp.where(tied, 1.0, 0.0)) > 0.5)
    def _():
        n_bits = max(1, (seq - 1).bit_length())

        def lim_body(j, lim):
            cand = lim + jnp.left_shift(jnp.int32(1), n_bits - 1 - j)
            return jnp.where(count(lambda k, p: (k == thr) & (p < cand)) < need, cand, lim)

        lim = lax.fori_loop(0, n_bits, lim_body, jnp.zeros((tq, 1), I32))
        lim_scr[...] = jnp.where(tied, lim, seq)

    thr = thr_scr[...]
    lim = lim_scr[...]

    m_scr[...] = jnp.full(m_scr.shape, MASKED_LOGIT, F32)
    l_scr[...] = jnp.zeros(l_scr.shape, F32)
    acc_scr[...] = jnp.zeros(acc_scr.shape, F32)

    def attn_body(kb, carry):
        off = pl.multiple_of(kb * tk, tk)
        logits = jnp.dot(qabs_scr[...], ckvt_ref[0, :, pl.ds(off, tk)], preferred_element_type=F32)
        keys = key_scr[:, pl.ds(off, tk)]
        k_pos = off + lax.broadcasted_iota(I32, (tq, tk), 1)
        sel = (keys >= thr) & ((keys > thr) | (k_pos <= lim))
        lg = jnp.where(sel[None], logits.reshape(h_n, tq, tk), MASKED_LOGIT)
        m_old = m_scr[...]
        m_new = jnp.maximum(m_old, jnp.max(lg, axis=-1, keepdims=True))
        a = jnp.exp(m_old - m_new)
        p = jnp.exp(lg - m_new)
        l_scr[...] = a * l_scr[...] + jnp.sum(p, axis=-1, keepdims=True)
        m_scr[...] = m_new
        pv = jnp.dot(p.astype(BF16).reshape(h_n * tq, tk), ckv_ref[0, pl.ds(off, tk), :],
                     preferred_element_type=F32)
        acc_scr[...] = acc_scr[...] * a + pv.reshape(h_n, tq, r_n)
        return carry

    lax.fori_loop(0, n_kv_blk, attn_body, 0)

    o_lat = (acc_scr[...] / l_scr[...]).astype(BF16)
    for h in range(h_n):
        o_scr[:, h * dh:(h + 1) * dh] = jnp.dot(o_lat[h], wuv_ref[h],
                                                 preferred_element_type=F32).astype(BF16)
    mix = jnp.dot(o_scr[...], wo_ref[...], preferred_element_type=F32)
    o_ref[0] = _layer_norm(alpha * x_ref[0] + mix, g_ref[...], b_ref[...])


def _dsa_attn(x, q, qi, wi, ckv, ckvt, kit, wuk, wuv, wo, g, b, alpha):
    bsz, seq, d = x.shape
    tq = Q_TILE
    topk = min(TOPK_MAX, seq // 4)
    h_n, r_n = ATTN_HEADS, KV_RANK
    qblk = lambda n: pl.BlockSpec((1, tq, n), lambda bi, i: (bi, i, 0))
    per_batch = lambda shp: pl.BlockSpec((1,) + shp, lambda bi, i: (bi, 0, 0), pipeline_mode=pl.Buffered(1))
    est = (2 * seq * r_n * 2 + seq * V7X_LANES * 2 + tq * seq * 4
           + (wuk.size + 2 * wuv.size + wo.size) * 2
           + h_n * tq * (r_n * 6 + 2 * V7X_LANES * 4)
           + IDX_HEADS * tq * IDX_KEY_TILE * 4 * 2 + h_n * tq * ATTN_KEY_TILE * 4 * 4
           + 6 * tq * d * 4 + (8 << 20))
    return pl.pallas_call(
        functools.partial(_dsa_kernel, topk=topk, seq=seq, alpha=alpha),
        grid=(bsz, seq // tq),
        in_specs=[qblk(d), qblk(q.shape[2]), qblk(qi.shape[2]), qblk(wi.shape[2]),
                  per_batch((seq, r_n)), per_batch((r_n, seq)), per_batch((IDX_HEAD_DIM, seq)),
                  _resident(wuk.shape), _resident(wuv.shape), _resident(wo.shape),
                  _resident(g.shape), _resident(b.shape)],
        out_specs=qblk(d),
        out_shape=jax.ShapeDtypeStruct((bsz, seq, d), F32),
        scratch_shapes=[pltpu.VMEM((tq, seq), I32),
                        pltpu.VMEM((h_n * tq, r_n), BF16),
                        pltpu.VMEM((IDX_HEADS * tq, IDX_HEAD_DIM), BF16),
                        pltpu.VMEM((h_n, tq, r_n), F32),
                        pltpu.VMEM((h_n, tq, 1), F32),
                        pltpu.VMEM((h_n, tq, 1), F32),
                        pltpu.VMEM((tq, h_n * ATTN_HEAD_DIM), BF16),
                        pltpu.VMEM((tq, 1), I32),
                        pltpu.VMEM((tq, 1), I32)],
        compiler_params=pltpu.CompilerParams(dimension_semantics=("arbitrary", "arbitrary"),
                                             vmem_limit_bytes=_vmem_limit(est)),
        name="dsa_attn",
    )(x, q, qi, wi, ckv, ckvt, kit, wuk, wuv, wo, g, b)


def _ssd_inproj_kernel(x_ref, wz_ref, wx_ref, wdt_ref, z_ref, xbc_ref, dt_ref):
    xb = x_ref[...].astype(BF16)
    z_ref[...] = jnp.dot(xb, wz_ref[...], preferred_element_type=F32)
    xbc_ref[...] = jnp.dot(xb, wx_ref[...], preferred_element_type=F32)
    dt_ref[...] = jnp.dot(xb, wdt_ref[...], preferred_element_type=F32)


def _ssd_inproj(x, wz, wx, wdt):
    t, d = x.shape
    tm = min(ROW_TILE, t)
    nz, nx, nd = wz.shape[1], wx.shape[1], wdt.shape[1]
    est = (wz.size + wx.size + wdt.size) * 2 + 2 * tm * d * 4 + 3 * tm * (nz + nx + nd) * 4 + (8 << 20)
    row = lambda n: pl.BlockSpec((tm, n), lambda i: (i, 0))
    return pl.pallas_call(
        _ssd_inproj_kernel,
        grid=(t // tm,),
        in_specs=[row(d), _resident(wz.shape), _resident(wx.shape), _resident(wdt.shape)],
        out_specs=[row(nz), row(nx), row(nd)],
        out_shape=[jax.ShapeDtypeStruct((t, nz), F32), jax.ShapeDtypeStruct((t, nx), F32),
                   jax.ShapeDtypeStruct((t, nd), F32)],
        compiler_params=pltpu.CompilerParams(dimension_semantics=("arbitrary",),
                                             vmem_limit_bytes=_vmem_limit(est)),
        name="ssd_inproj",
    )(x, wz, wx, wdt)


def _cumsum(a, axis):
    n = a.shape[axis]
    idx = lax.broadcasted_iota(I32, a.shape, axis)
    s = 1
    while s < n:
        a = a + jnp.where(idx >= s, pltpu.roll(a, s, axis), 0.0)
        s *= 2
    return a


def _softplus(v):
    return jnp.maximum(v, 0.0) + jnp.log1p(jnp.exp(-jnp.abs(v)))


def _ssd_scan_kernel(xbc_ref, z_ref, dt_ref, dtt_ref, cw_ref, cb_ref, dtb_ref, dtbt_ref, alog_ref, alogt_ref,
                     dsk_ref, ng_ref, o_ref,
                     win_scr, state_scr, xdte_scr, y_scr, cd_scr, *, inner, heads):
    c_idx = pl.program_id(1)
    ln = SSD_CHUNK
    g_n, n_st, p_dim = SSM_GROUPS, SSM_STATE, SSM_HEAD_DIM
    gw = inner // g_n
    pairs_per_group = gw // V7X_LANES
    tail = V7X_SUBLANES

    @pl.when(c_idx == 0)
    def _():
        win_scr[0:tail, :] = jnp.zeros((tail, win_scr.shape[1]), F32)
        state_scr[...] = jnp.zeros(state_scr.shape, F32)

    xbc = xbc_ref[0]
    win_scr[tail:tail + ln, :] = xbc
    conv = cb_ref[...] + cw_ref[0:1, :] * win_scr[tail - SSM_CONV + 1:tail - SSM_CONV + 1 + ln, :]
    for k in range(1, SSM_CONV):
        lo = tail - SSM_CONV + 1 + k
        conv = conv + cw_ref[k:k + 1, :] * win_scr[lo:lo + ln, :]
    win_scr[0:tail, :] = xbc[ln - tail:ln, :]
    act = _silu(conv)
    xs = act[:, :inner]
    bm = act[:, inner:inner + g_n * n_st]
    cm = act[:, inner + g_n * n_st:]

    a_row = -jnp.exp(alog_ref[...])
    a_col = -jnp.exp(alogt_ref[...])
    dt = _softplus(dt_ref[0] + dtb_ref[...])
    dtt = _softplus(dtt_ref[0] + dtbt_ref[...])
    acs = _cumsum(dt * a_row, 0)
    acst = _cumsum(dtt * a_col, 1)
    e_acs = jnp.exp(acs)
    last = acs[ln - 1:ln, :]
    e_end = jnp.exp(last - acs)
    e_last = jnp.exp(last)

    tri = lax.broadcasted_iota(I32, (ln, ln), 0) >= lax.broadcasted_iota(I32, (ln, ln), 1)
    low = lax.broadcasted_iota(I32, (ln, V7X_LANES), 1) < p_dim
    low1 = low[0:1, :]

    def pair_cols(v, h0):
        return jnp.where(low, v[:, h0:h0 + 1], v[:, h0 + 1:h0 + 2])

    for g in range(g_n):
        bm_g = bm[:, g * n_st:(g + 1) * n_st]
        cm_g = cm[:, g * n_st:(g + 1) * n_st].astype(BF16)
        bmt_g = bm_g.T.astype(BF16)
        cb = jnp.dot(cm_g, bmt_g, preferred_element_type=F32)
        y_off = jnp.dot(cm_g, state_scr[g].astype(BF16), preferred_element_type=F32)
        for jp in range(pairs_per_group):
            j = g * pairs_per_group + jp
            h0 = 2 * j
            lanes = slice(j * V7X_LANES, (j + 1) * V7X_LANES)
            d0 = jnp.where(tri, jnp.exp(acs[:, h0:h0 + 1] - acst[h0:h0 + 1, :]), 0.0)
            d1 = jnp.where(tri, jnp.exp(acs[:, h0 + 1:h0 + 2] - acst[h0 + 1:h0 + 2, :]), 0.0)
            m_pair = jnp.concatenate([(cb * d0).astype(BF16), (cb * d1).astype(BF16)], axis=1)
            x_pair = xs[:, lanes]
            xdt = x_pair * pair_cols(dt, h0)
            xb = xdt.astype(BF16)
            zero = jnp.zeros_like(xb)
            rhs = jnp.concatenate([jnp.where(low, xb, zero), jnp.where(low, zero, xb)], axis=0)
            y_diag = jnp.dot(m_pair, rhs, preferred_element_type=F32)
            y = y_diag + y_off[:, jp * V7X_LANES:(jp + 1) * V7X_LANES] * pair_cols(e_acs, h0)
            y_scr[:, lanes] = y + x_pair * dsk_ref[:, lanes]
            xdte_scr[:, lanes] = (xdt * pair_cols(e_end, h0)).astype(BF16)
            cd_scr[:, lanes] = jnp.where(low1, e_last[:, h0:h0 + 1], e_last[:, h0 + 1:h0 + 2])
        new = jnp.dot(bmt_g, xdte_scr[:, g * gw:(g + 1) * gw], preferred_element_type=F32)
        state_scr[g] = state_scr[g] * cd_scr[:, g * gw:(g + 1) * gw] + new

    yz = y_scr[...] * _silu(z_ref[0])
    for g in range(g_n):
        blk = yz[:, g * gw:(g + 1) * gw]
        ms = jnp.mean(blk * blk, axis=-1, keepdims=True)
        o_ref[0, :, g * gw:(g + 1) * gw] = (blk * lax.rsqrt(ms + RMS_EPS)
                                            * ng_ref[:, g * gw:(g + 1) * gw]).astype(BF16)


def _ssd_scan(xbc, z, dt, dtt, cw, cb, dtb, dtbt, alog, alogt, dsk, ng, heads):
    bsz, seq, conv_dim = xbc.shape
    inner = z.shape[2]
    ln = SSD_CHUNK
    lanes = dt.shape[2]
    gw = inner // SSM_GROUPS
    blk = lambda n: pl.BlockSpec((1, ln, n), lambda bi, c: (bi, c, 0))
    est = (2 * ln * (conv_dim + inner) * 4 + 2 * ln * inner * 2 + (ln + 8) * conv_dim * 4
           + SSM_GROUPS * SSM_STATE * gw * 4 + ln * inner * 6 + 12 * ln * conv_dim * 4 + (8 << 20))
    return pl.pallas_call(
        functools.partial(_ssd_scan_kernel, inner=inner, heads=heads),
        grid=(bsz, seq // ln),
        in_specs=[blk(conv_dim), blk(inner), blk(lanes),
                  pl.BlockSpec((1, lanes, ln), lambda bi, c: (bi, 0, c)),
                  _resident(cw.shape), _resident(cb.shape), _resident(dtb.shape), _resident(dtbt.shape),
                  _resident(alog.shape), _resident(alogt.shape), _resident(dsk.shape), _resident(ng.shape)],
        out_specs=blk(inner),
        out_shape=jax.ShapeDtypeStruct((bsz, seq, inner), BF16),
        scratch_shapes=[pltpu.VMEM((ln + V7X_SUBLANES, conv_dim), F32),
                        pltpu.VMEM((SSM_GROUPS, SSM_STATE, gw), F32),
                        pltpu.VMEM((ln, inner), BF16),
                        pltpu.VMEM((ln, inner), F32),
                        pltpu.VMEM((1, inner), F32)],
        compiler_params=pltpu.CompilerParams(dimension_semantics=("arbitrary", "arbitrary"),
                                             vmem_limit_bytes=_vmem_limit(est)),
        name="ssd_scan",
    )(xbc, z, dt, dtt, cw, cb, dtb, dtbt, alog, alogt, dsk, ng)


def _ffn_weights(w_in, w_out):
    n_l, d, f2 = w_in.shape
    f = f2 // 2
    ck = V7X_MXU_DIM if f % V7X_MXU_DIM == 0 else V7X_LANES
    n = f // ck
    win = w_in.astype(BF16).reshape(n_l, d, 2, n, ck).transpose(0, 2, 3, 1, 4).reshape(n_l, 2 * n, d, ck)
    wout = w_out.astype(BF16).reshape(n_l, n, ck, d)
    return win, wout


def _dsa_layer(xt, bsz, seq, w_in, kvg, w_uk, w_uv, w_out, g, b, alpha):
    d = xt.shape[1]
    o1 = ATTN_HEADS * ATTN_HEAD_DIM
    o2 = o1 + KV_RANK
    o3 = o2 + IDX_HEADS * IDX_HEAD_DIM
    wb = w_in.astype(BF16)
    pad = V7X_LANES - IDX_HEAD_DIM - IDX_HEADS
    wkw = jnp.pad(wb[:, o3:], ((0, 0), (0, pad)))
    q, ckv, qi, ki, wi = _attn_inproj(xt, wb[:, :o1], wb[:, o1:o2], wb[:, o2:o3], wkw, kvg.reshape(1, -1))
    ckv = ckv.reshape(bsz, seq, KV_RANK)
    out = _dsa_attn(xt.reshape(bsz, seq, d), q.reshape(bsz, seq, -1), qi.reshape(bsz, seq, -1),
                    wi.reshape(bsz, seq, -1), ckv, jnp.swapaxes(ckv, 1, 2),
                    jnp.swapaxes(ki.reshape(bsz, seq, -1), 1, 2),
                    w_uk.astype(BF16), w_uv.astype(BF16), w_out.astype(BF16), g, b, alpha)
    return out.reshape(bsz * seq, d)


def _ssd_layer(xt, bsz, seq, w_in, conv_w, conv_b, dt_bias, a_log, d_skip, norm_g, w_out, g, b, alpha):
    inner = norm_g.shape[0]
    heads = dt_bias.shape[0]
    conv_dim = conv_w.shape[1]
    lanes = V7X_LANES * pl.cdiv(heads, V7X_LANES)
    wb = w_in.astype(BF16)
    wdt = jnp.pad(wb[:, inner + conv_dim:], ((0, 0), (0, lanes - heads)))
    z, xbc, dt = _ssd_inproj(xt, wb[:, :inner], wb[:, inner:inner + conv_dim], wdt)
    dt = dt.reshape(bsz, seq, lanes)
    padh = lambda v: jnp.pad(v, (0, lanes - heads))
    dtb, alog = padh(dt_bias), padh(a_log)
    yz = _ssd_scan(xbc.reshape(bsz, seq, conv_dim), z.reshape(bsz, seq, inner), dt, jnp.swapaxes(dt, 1, 2),
                   conv_w, conv_b.reshape(1, -1), dtb.reshape(1, -1), dtb.reshape(-1, 1),
                   alog.reshape(1, -1), alog.reshape(-1, 1),
                   jnp.repeat(d_skip, SSM_HEAD_DIM).reshape(1, -1), norm_g.reshape(1, -1), heads)
    return _proj_ln(xt, yz.reshape(bsz * seq, inner), w_out.astype(BF16), g, b, alpha)


def kernel(x, ln_g, ln_b, ffn1_w_in, ffn1_w_out, ffn2_w_in, ffn2_w_out, attn_w_in, attn_kv_norm, attn_w_uk,
           attn_w_uv, attn_w_out, ssm_w_in, ssm_conv_w, ssm_conv_b, ssm_dt_bias, ssm_a_log, ssm_d,
           ssm_norm_g, ssm_w_out):
    bsz, seq, d = x.shape
    depth = ln_g.shape[0]
    alpha = (2.0 * depth) ** 0.25
    f1_in, f1_out = _ffn_weights(ffn1_w_in, ffn1_w_out)
    f2_in, f2_out = _ffn_weights(ffn2_w_in, ffn2_w_out)
    ln = lambda i, k: (ln_g[i, k].reshape(1, d), ln_b[i, k].reshape(1, d))
    xt = x.reshape(bsz * seq, d)
    for i in range(depth):
        xt = _ffn_ln(xt, f1_in[i], f1_out[i], *ln(i, 0), alpha)
        j = i // 2
        if i % 2 == 0:
            xt = _dsa_layer(xt, bsz, seq, attn_w_in[j], attn_kv_norm[j], attn_w_uk[j], attn_w_uv[j],
                            attn_w_out[j], *ln(i, 1), alpha)
        else:
            xt = _ssd_layer(xt, bsz, seq, ssm_w_in[j], ssm_conv_w[j], ssm_conv_b[j], ssm_dt_bias[j],
                            ssm_a_log[j], ssm_d[j], ssm_norm_g[j], ssm_w_out[j], *ln(i, 1), alpha)
        xt = _ffn_ln(xt, f2_in[i], f2_out[i], *ln(i, 2), alpha)
    return xt.reshape(bsz, seq, d)
```

```python
import functools
import math

import jax
import jax.numpy as jnp
from jax import lax
from jax.experimental import pallas as pl
from jax.experimental.pallas import tpu as pltpu

F32 = jnp.float32
BF16 = jnp.bfloat16
I32 = jnp.int32

ATTN_HEADS = 16
ATTN_HEAD_DIM = 64
KV_RANK = 256
IDX_HEADS = 8
IDX_HEAD_DIM = 64
TOPK_MAX = 256
SSM_HEAD_DIM = 64
SSM_GROUPS = 4
SSM_STATE = 128
SSM_CONV = 4
LN_EPS = 1e-5
RMS_EPS = 1e-6

V7X_LANES = 128
V7X_SUBLANES = 8
V7X_MXU_DIM = 256
V7X_VMEM_BYTES = 64 * 1024 * 1024

ROW_TILE = 512
Q_TILE = 128
IDX_KEY_TILE = 512
ATTN_KEY_TILE = V7X_MXU_DIM
COUNT_ROWS = 64
SSD_CHUNK = 128
MASKED_LOGIT = -1e30
INT_MIN = -(2 ** 31)
NEG_INF_KEY = INT_MIN + 0x007FFFFF


def _vmem_limit(nbytes):
    return int(min(nbytes, V7X_VMEM_BYTES - 8 * 1024 * 1024))


def _resident(shape):
    nd = len(shape)
    return pl.BlockSpec(shape, lambda *_: (0,) * nd, pipeline_mode=pl.Buffered(1))


def _layer_norm(y, g, b):
    mu = jnp.mean(y, axis=-1, keepdims=True)
    d = y - mu
    var = jnp.mean(d * d, axis=-1, keepdims=True)
    return d * lax.rsqrt(var + LN_EPS) * g + b


def _silu(v):
    return v * jax.nn.sigmoid(v)


def _ffn_ln_kernel(x_ref, win_ref, wout_ref, g_ref, b_ref, o_ref, acc_ref, *, n_chunk, alpha):
    x = x_ref[...]
    xb = x.astype(BF16)
    for c in range(n_chunk):
        gate = jnp.dot(xb, win_ref[c], preferred_element_type=F32)
        up = jnp.dot(xb, win_ref[n_chunk + c], preferred_element_type=F32)
        act = (_silu(gate) * up).astype(BF16)
        part = jnp.dot(act, wout_ref[c], preferred_element_type=F32)
        if c == 0:
            acc_ref[...] = part
        else:
            acc_ref[...] += part
    y = alpha * x + 0.5 * acc_ref[...]
    o_ref[...] = _layer_norm(y, g_ref[...], b_ref[...])


def _ffn_ln(x, win, wout, g, b, alpha):
    t, d = x.shape
    n2, _, ck = win.shape
    n_chunk = n2 // 2
    tm = min(ROW_TILE, t)
    est = (win.size + wout.size) * 2 + 5 * tm * d * 4 + 6 * tm * ck * 4 + (8 << 20)
    return pl.pallas_call(
        functools.partial(_ffn_ln_kernel, n_chunk=n_chunk, alpha=alpha),
        grid=(t // tm,),
        in_specs=[pl.BlockSpec((tm, d), lambda i: (i, 0)),
                  _resident(win.shape), _resident(wout.shape),
                  _resident(g.shape), _resident(b.shape)],
        out_specs=pl.BlockSpec((tm, d), lambda i: (i, 0)),
        out_shape=jax.ShapeDtypeStruct((t, d), F32),
        scratch_shapes=[pltpu.VMEM((tm, d), F32)],
        compiler_params=pltpu.CompilerParams(dimension_semantics=("arbitrary",),
                                             vmem_limit_bytes=_vmem_limit(est)),
        name="ffn_ln",
    )(x, win, wout, g, b)


def _proj_ln_kernel(x_ref, a_ref, w_ref, g_ref, b_ref, o_ref, *, alpha):
    m = jnp.dot(a_ref[...], w_ref[...], preferred_element_type=F32)
    o_ref[...] = _layer_norm(alpha * x_ref[...] + m, g_ref[...], b_ref[...])


def _proj_ln(x, a, w, g, b, alpha):
    t, d = x.shape
    k = a.shape[1]
    tm = min(ROW_TILE, t)
    est = w.size * 2 + 6 * tm * d * 4 + 2 * tm * k * 2 + (8 << 20)
    return pl.pallas_call(
        functools.partial(_proj_ln_kernel, alpha=alpha),
        grid=(t // tm,),
        in_specs=[pl.BlockSpec((tm, d), lambda i: (i, 0)),
                  pl.BlockSpec((tm, k), lambda i: (i, 0)),
                  _resident(w.shape), _resident(g.shape), _resident(b.shape)],
        out_specs=pl.BlockSpec((tm, d), lambda i: (i, 0)),
        out_shape=jax.ShapeDtypeStruct((t, d), F32),
        compiler_params=pltpu.CompilerParams(dimension_semantics=("arbitrary",),
                                             vmem_limit_bytes=_vmem_limit(est)),
        name="proj_ln",
    )(x, a, w, g, b)


def _attn_inproj_kernel(x_ref, wq_ref, wc_ref, wqi_ref, wkw_ref, kvg_ref,
                        q_ref, c_ref, qi_ref, ki_ref, wi_ref, *, w_scale):
    xb = x_ref[...].astype(BF16)
    q_ref[...] = jnp.dot(xb, wq_ref[...], preferred_element_type=F32).astype(BF16)
    c = jnp.dot(xb, wc_ref[...], preferred_element_type=F32)
    ms = jnp.mean(c * c, axis=-1, keepdims=True)
    c_ref[...] = (c * lax.rsqrt(ms + RMS_EPS) * kvg_ref[...]).astype(BF16)
    qi_ref[...] = jnp.dot(xb, wqi_ref[...], preferred_element_type=F32).astype(BF16)
    kw = jnp.dot(xb, wkw_ref[...], preferred_element_type=F32)
    ki_ref[...] = kw[:, :IDX_HEAD_DIM].astype(BF16)
    wi_ref[...] = kw[:, IDX_HEAD_DIM:IDX_HEAD_DIM + IDX_HEADS] * (IDX_HEADS ** -0.5) * w_scale


def _attn_inproj(x, wq, wc, wqi, wkw, kvg):
    t, d = x.shape
    tm = min(ROW_TILE, t)
    nq, nc, nqi = wq.shape[1], wc.shape[1], wqi.shape[1]
    est = (wq.size + wc.size + wqi.size + wkw.size) * 2 + 2 * tm * d * 4 + 8 * tm * (nq + nc + nqi) + (8 << 20)
    row = lambda n: pl.BlockSpec((tm, n), lambda i: (i, 0))
    return pl.pallas_call(
        functools.partial(_attn_inproj_kernel, w_scale=IDX_HEAD_DIM ** -0.5),
        grid=(t // tm,),
        in_specs=[row(d), _resident(wq.shape), _resident(wc.shape), _resident(wqi.shape),
                  _resident(wkw.shape), _resident(kvg.shape)],
        out_specs=[row(nq), row(nc), row(nqi), row(IDX_HEAD_DIM), row(IDX_HEADS)],
        out_shape=[jax.ShapeDtypeStruct((t, nq), BF16), jax.ShapeDtypeStruct((t, nc), BF16),
                   jax.ShapeDtypeStruct((t, nqi), BF16), jax.ShapeDtypeStruct((t, IDX_HEAD_DIM), BF16),
                   jax.ShapeDtypeStruct((t, IDX_HEADS), F32)],
        compiler_params=pltpu.CompilerParams(dimension_semantics=("arbitrary",),
                                             vmem_limit_bytes=_vmem_limit(est)),
        name="attn_inproj",
    )(x, wq, wc, wqi, wkw, kvg)


def _dsa_kernel(x_ref, qt_ref, qit_ref, wit_ref, ckv_ref, ckvt_ref, ki_ref, wukt_ref, wuvt_ref, wo_ref,
                g_ref, b_ref, o_ref,
                key_scr, qabs_scr, qicat_scr, acc_scr, m_scr, l_scr, ot_scr, thr_scr, lim_scr,
                *, topk, seq, alpha):
    i = pl.program_id(1)
    tq, h_n, dh, r_n = Q_TILE, ATTN_HEADS, ATTN_HEAD_DIM, KV_RANK
    hi_n, di = IDX_HEADS, IDX_HEAD_DIM
    tki, tk = IDX_KEY_TILE, ATTN_KEY_TILE
    pair = 2 * tq
    q0 = i * tq
    n_idx_blk = q0 // tki + 1
    n_kv_blk = q0 // tk + 1

    for h in range(h_n):
        qa = jnp.dot(wukt_ref[h], qt_ref[0, h * dh:(h + 1) * dh, :], preferred_element_type=F32)
        qabs_scr[:, h * tq:(h + 1) * tq] = (qa * (dh ** -0.5)).astype(BF16)
    for h in range(hi_n):
        qicat_scr[:, h * tq:(h + 1) * tq] = qit_ref[0, h * di:(h + 1) * di, :]

    wit = wit_ref[0]
    q_pos = q0 + lax.broadcasted_iota(I32, (1, tq), 1)

    def score_body(kb, carry):
        off = pl.multiple_of(kb * tki, tki)
        ki = ki_ref[0, pl.ds(off, tki), :]
        score = jnp.zeros((tki, tq), F32)
        for j in range(hi_n // 2):
            s = jnp.dot(ki, qicat_scr[:, j * pair:(j + 1) * pair], preferred_element_type=F32)
            for e in range(2):
                h = 2 * j + e
                score = score + jnp.maximum(s[:, e * tq:(e + 1) * tq], 0.0) * wit[h:h + 1, :]
        k_pos = off + lax.broadcasted_iota(I32, (tki, tq), 0)
        score = jnp.where(k_pos <= q_pos, score, -jnp.inf)
        bits = pltpu.bitcast(score, I32)
        key_scr[pl.ds(off, tki), :] = bits ^ ((bits >> 31) & 0x7FFFFFFF)
        return carry

    lax.fori_loop(0, n_idx_blk, score_body, 0)

    def count(pred):
        def body(kb, c):
            off = pl.multiple_of(kb * tki, tki)
            keys = key_scr[pl.ds(off, tki), :]
            k_pos = off + lax.broadcasted_iota(I32, (tki, tq), 0)
            hit = jnp.where(pred(keys, k_pos), 1.0, 0.0)
            return c + jnp.sum(hit.reshape(tki // COUNT_ROWS, COUNT_ROWS, tq), axis=0)
        c = lax.fori_loop(0, n_idx_blk, body, jnp.zeros((COUNT_ROWS, tq), F32))
        return jnp.sum(c, axis=0, keepdims=True)

    kf = float(topk)

    thr0 = jnp.where(count(lambda k, p: k >= 0) >= kf, jnp.zeros((1, tq), I32),
                     jnp.full((1, tq), INT_MIN, I32))

    def thr_body(j, thr):
        cand = thr + jnp.left_shift(jnp.int32(1), 30 - j)
        return jnp.where(count(lambda k, p: k >= cand) >= kf, cand, thr)

    thr = lax.fori_loop(0, 31, thr_body, thr0)
    thr = jnp.maximum(thr, NEG_INF_KEY + 1)
    n_ge = count(lambda k, p: k >= thr)
    n_gt = count(lambda k, p: k > thr)
    tied = n_ge > kf
    need = kf - n_gt
    thr_scr[...] = thr
    lim_scr[...] = jnp.full((1, tq), seq, I32)

    @pl.when(jnp.max(jnp.where(tied, 1.0, 0.0)) > 0.5)
    def _():
        n_bits = max(1, (seq - 1).bit_length())

        def lim_body(j, lim):
            cand = lim + jnp.left_shift(jnp.int32(1), n_bits - 1 - j)
            return jnp.where(count(lambda k, p: (k == thr) & (p < cand)) < need, cand, lim)

        lim = lax.fori_loop(0, n_bits, lim_body, jnp.zeros((1, tq), I32))
        lim_scr[...] = jnp.where(tied, lim, seq)

    thr = thr_scr[...]
    lim = lim_scr[...]

    m_scr[...] = jnp.full(m_scr.shape, MASKED_LOGIT, F32)
    l_scr[...] = jnp.zeros(l_scr.shape, F32)
    acc_scr[...] = jnp.zeros(acc_scr.shape, F32)

    def attn_body(kb, carry):
        off = pl.multiple_of(kb * tk, tk)
        keys = key_scr[pl.ds(off, tk), :]
        k_pos = off + lax.broadcasted_iota(I32, (tk, tq), 0)
        sel = (keys >= thr) & ((keys > thr) | (k_pos <= lim))
        bias = jnp.where(sel, 0.0, MASKED_LOGIT)
        bias = jnp.concatenate([bias, bias], axis=1)
        c_blk = ckv_ref[0, pl.ds(off, tk), :]
        ct_blk = ckvt_ref[0, :, pl.ds(off, tk)]
        for j in range(h_n // 2):
            cols = slice(j * pair, (j + 1) * pair)
            lg = jnp.dot(c_blk, qabs_scr[:, cols], preferred_element_type=F32) + bias
            m_old = m_scr[:, cols]
            m_new = jnp.maximum(m_old, jnp.max(lg, axis=0, keepdims=True))
            a = jnp.exp(m_old - m_new)
            p = jnp.exp(lg - m_new)
            l_scr[:, cols] = a * l_scr[:, cols] + jnp.sum(p, axis=0, keepdims=True)
            m_scr[:, cols] = m_new
            pv = jnp.dot(ct_blk, p.astype(BF16), preferred_element_type=F32)
            acc_scr[:, cols] = acc_scr[:, cols] * a + pv
        return carry

    lax.fori_loop(0, n_kv_blk, attn_body, 0)

    for h in range(h_n):
        cols = slice(h * tq, (h + 1) * tq)
        o_lat = (acc_scr[:, cols] / l_scr[:, cols]).astype(BF16)
        ot_scr[h * dh:(h + 1) * dh, :] = jnp.dot(wuvt_ref[h], o_lat, preferred_element_type=F32)
    o = ot_scr[...].T.astype(BF16)
    mix = jnp.dot(o, wo_ref[...], preferred_element_type=F32)
    o_ref[0] = _layer_norm(alpha * x_ref[0] + mix, g_ref[...], b_ref[...])


def _dsa_attn(x, qt, qit, wit, ckv, ckvt, ki, wukt, wuvt, wo, g, b, alpha):
    bsz, seq, d = x.shape
    tq = Q_TILE
    topk = min(TOPK_MAX, seq // 4)
    h_n, r_n = ATTN_HEADS, KV_RANK
    qblk = lambda n: pl.BlockSpec((1, tq, n), lambda bi, i: (bi, i, 0))
    qblk_t = lambda n: pl.BlockSpec((1, n, tq), lambda bi, i: (bi, 0, i))
    per_batch = lambda shp: pl.BlockSpec((1,) + shp, lambda bi, i: (bi, 0, 0), pipeline_mode=pl.Buffered(1))
    est = (2 * seq * r_n * 2 + seq * V7X_LANES * 2 + tq * seq * 4
           + (wukt.size + wuvt.size + wo.size) * 2
           + h_n * tq * r_n * 6 + 2 * IDX_KEY_TILE * 2 * tq * 4 * 2 + 6 * ATTN_KEY_TILE * 2 * tq * 4
           + 2 * h_n * ATTN_HEAD_DIM * tq * (2 + 4) + 6 * tq * d * 4 + (8 << 20))
    return pl.pallas_call(
        functools.partial(_dsa_kernel, topk=topk, seq=seq, alpha=alpha),
        grid=(bsz, seq // tq),
        in_specs=[qblk(d), qblk_t(qt.shape[1]), qblk_t(qit.shape[1]), qblk_t(wit.shape[1]),
                  per_batch((seq, r_n)), per_batch((r_n, seq)), per_batch((seq, IDX_HEAD_DIM)),
                  _resident(wukt.shape), _resident(wuvt.shape), _resident(wo.shape),
                  _resident(g.shape), _resident(b.shape)],
        out_specs=qblk(d),
        out_shape=jax.ShapeDtypeStruct((bsz, seq, d), F32),
        scratch_shapes=[pltpu.VMEM((seq, tq), I32),
                        pltpu.VMEM((r_n, h_n * tq), BF16),
                        pltpu.VMEM((IDX_HEAD_DIM, IDX_HEADS * tq), BF16),
                        pltpu.VMEM((r_n, h_n * tq), F32),
                        pltpu.VMEM((1, h_n * tq), F32),
                        pltpu.VMEM((1, h_n * tq), F32),
                        pltpu.VMEM((h_n * ATTN_HEAD_DIM, tq), F32),
                        pltpu.VMEM((1, tq), I32),
                        pltpu.VMEM((1, tq), I32)],
        compiler_params=pltpu.CompilerParams(dimension_semantics=("arbitrary", "arbitrary"),
                                             vmem_limit_bytes=_vmem_limit(est)),
        name="dsa_attn",
    )(x, qt, qit, wit, ckv, ckvt, ki, wukt, wuvt, wo, g, b)


def _ssd_inproj_kernel(x_ref, wz_ref, wx_ref, wdt_ref, z_ref, xbc_ref, dt_ref):
    xb = x_ref[...].astype(BF16)
    z_ref[...] = jnp.dot(xb, wz_ref[...], preferred_element_type=F32)
    xbc_ref[...] = jnp.dot(xb, wx_ref[...], preferred_element_type=F32)
    dt_ref[...] = jnp.dot(xb, wdt_ref[...], preferred_element_type=F32)


def _ssd_inproj(x, wz, wx, wdt):
    t, d = x.shape
    tm = min(ROW_TILE, t)
    nz, nx, nd = wz.shape[1], wx.shape[1], wdt.shape[1]
    est = (wz.size + wx.size + wdt.size) * 2 + 2 * tm * d * 4 + 3 * tm * (nz + nx + nd) * 4 + (8 << 20)
    row = lambda n: pl.BlockSpec((tm, n), lambda i: (i, 0))
    return pl.pallas_call(
        _ssd_inproj_kernel,
        grid=(t // tm,),
        in_specs=[row(d), _resident(wz.shape), _resident(wx.shape), _resident(wdt.shape)],
        out_specs=[row(nz), row(nx), row(nd)],
        out_shape=[jax.ShapeDtypeStruct((t, nz), F32), jax.ShapeDtypeStruct((t, nx), F32),
                   jax.ShapeDtypeStruct((t, nd), F32)],
        compiler_params=pltpu.CompilerParams(dimension_semantics=("arbitrary",),
                                             vmem_limit_bytes=_vmem_limit(est)),
        name="ssd_inproj",
    )(x, wz, wx, wdt)


def _cumsum(a, axis):
    n = a.shape[axis]
    idx = lax.broadcasted_iota(I32, a.shape, axis)
    s = 1
    while s < n:
        a = a + jnp.where(idx >= s, pltpu.roll(a, s, axis), 0.0)
        s *= 2
    return a


def _softplus(v):
    return jnp.maximum(v, 0.0) + jnp.log1p(jnp.exp(-jnp.abs(v)))


def _ssd_scan_kernel(xbc_ref, z_ref, dt_ref, dtt_ref, cw_ref, cb_ref, dtb_ref, dtbt_ref, alog_ref, alogt_ref,
                     dsk_ref, ng_ref, o_ref,
                     win_scr, state_scr, xdte_scr, y_scr, cd_scr, *, inner, heads):
    c_idx = pl.program_id(1)
    ln = SSD_CHUNK
    g_n, n_st, p_dim = SSM_GROUPS, SSM_STATE, SSM_HEAD_DIM
    gw = inner // g_n
    pairs_per_group = gw // V7X_LANES
    tail = V7X_SUBLANES

    @pl.when(c_idx == 0)
    def _():
        win_scr[0:tail, :] = jnp.zeros((tail, win_scr.shape[1]), F32)
        state_scr[...] = jnp.zeros(state_scr.shape, F32)

    xbc = xbc_ref[0]
    win_scr[tail:tail + ln, :] = xbc
    conv = cb_ref[...] + cw_ref[0:1, :] * win_scr[tail - SSM_CONV + 1:tail - SSM_CONV + 1 + ln, :]
    for k in range(1, SSM_CONV):
        lo = tail - SSM_CONV + 1 + k
        conv = conv + cw_ref[k:k + 1, :] * win_scr[lo:lo + ln, :]
    win_scr[0:tail, :] = xbc[ln - tail:ln, :]
    act = _silu(conv)
    xs = act[:, :inner]
    bm = act[:, inner:inner + g_n * n_st]
    cm = act[:, inner + g_n * n_st:]

    a_row = -jnp.exp(alog_ref[...])
    a_col = -jnp.exp(alogt_ref[...])
    dt = _softplus(dt_ref[0] + dtb_ref[...])
    dtt = _softplus(dtt_ref[0] + dtbt_ref[...])
    acs = _cumsum(dt * a_row, 0)
    acst = _cumsum(dtt * a_col, 1)
    e_acs = jnp.exp(acs)
    last = acs[ln - 1:ln, :]
    e_end = jnp.exp(last - acs)
    e_last = jnp.exp(last)

    tri = lax.broadcasted_iota(I32, (ln, ln), 0) >= lax.broadcasted_iota(I32, (ln, ln), 1)
    low = lax.broadcasted_iota(I32, (ln, V7X_LANES), 1) < p_dim
    low1 = low[0:1, :]

    def pair_cols(v, h0):
        return jnp.where(low, v[:, h0:h0 + 1], v[:, h0 + 1:h0 + 2])

    for g in range(g_n):
        bm_g = bm[:, g * n_st:(g + 1) * n_st]
        cm_g = cm[:, g * n_st:(g + 1) * n_st].astype(BF16)
        bmt_g = bm_g.T.astype(BF16)
        cb = jnp.dot(cm_g, bmt_g, preferred_element_type=F32)
        y_off = jnp.dot(cm_g, state_scr[g].astype(BF16), preferred_element_type=F32)
        for jp in range(pairs_per_group):
            j = g * pairs_per_group + jp
            h0 = 2 * j
            lanes = slice(j * V7X_LANES, (j + 1) * V7X_LANES)
            d0 = jnp.where(tri, jnp.exp(acs[:, h0:h0 + 1] - acst[h0:h0 + 1, :]), 0.0)
            d1 = jnp.where(tri, jnp.exp(acs[:, h0 + 1:h0 + 2] - acst[h0 + 1:h0 + 2, :]), 0.0)
            m_pair = jnp.concatenate([(cb * d0).astype(BF16), (cb * d1).astype(BF16)], axis=1)
            x_pair = xs[:, lanes]
            xdt = x_pair * pair_cols(dt, h0)
            xb = xdt.astype(BF16)
            zero = jnp.zeros_like(xb)
            rhs = jnp.concatenate([jnp.where(low, xb, zero), jnp.where(low, zero, xb)], axis=0)
            y_diag = jnp.dot(m_pair, rhs, preferred_element_type=F32)
            y = y_diag + y_off[:, jp * V7X_LANES:(jp + 1) * V7X_LANES] * pair_cols(e_acs, h0)
            y_scr[:, lanes] = y + x_pair * dsk_ref[:, lanes]
            xdte_scr[:, lanes] = (xdt * pair_cols(e_end, h0)).astype(BF16)
            cd_scr[:, lanes] = jnp.where(low1, e_last[:, h0:h0 + 1], e_last[:, h0 + 1:h0 + 2])
        new = jnp.dot(bmt_g, xdte_scr[:, g * gw:(g + 1) * gw], preferred_element_type=F32)
        state_scr[g] = state_scr[g] * cd_scr[:, g * gw:(g + 1) * gw] + new

    yz = y_scr[...] * _silu(z_ref[0])
    for g in range(g_n):
        blk = yz[:, g * gw:(g + 1) * gw]
        ms = jnp.mean(blk * blk, axis=-1, keepdims=True)
        o_ref[0, :, g * gw:(g + 1) * gw] = (blk * lax.rsqrt(ms + RMS_EPS)
                                            * ng_ref[:, g * gw:(g + 1) * gw]).astype(BF16)


def _ssd_scan(xbc, z, dt, dtt, cw, cb, dtb, dtbt, alog, alogt, dsk, ng, heads):
    bsz, seq, conv_dim = xbc.shape
    inner = z.shape[2]
    ln = SSD_CHUNK
    lanes = dt.shape[2]
    gw = inner // SSM_GROUPS
    blk = lambda n: pl.BlockSpec((1, ln, n), lambda bi, c: (bi, c, 0))
    est = (2 * ln * (conv_dim + inner) * 4 + 2 * ln * inner * 2 + (ln + 8) * conv_dim * 4
           + SSM_GROUPS * SSM_STATE * gw * 4 + ln * inner * 6 + 12 * ln * conv_dim * 4 + (8 << 20))
    return pl.pallas_call(
        functools.partial(_ssd_scan_kernel, inner=inner, heads=heads),
        grid=(bsz, seq // ln),
        in_specs=[blk(conv_dim), blk(inner), blk(lanes),
                  pl.BlockSpec((1, lanes, ln), lambda bi, c: (bi, 0, c)),
                  _resident(cw.shape), _resident(cb.shape), _resident(dtb.shape), _resident(dtbt.shape),
                  _resident(alog.shape), _resident(alogt.shape), _resident(dsk.shape), _resident(ng.shape)],
        out_specs=blk(inner),
        out_shape=jax.ShapeDtypeStruct((bsz, seq, inner), BF16),
        scratch_shapes=[pltpu.VMEM((ln + V7X_SUBLANES, conv_dim), F32),
                        pltpu.VMEM((SSM_GROUPS, SSM_STATE, gw), F32),
                        pltpu.VMEM((ln, inner), BF16),
                        pltpu.VMEM((ln, inner), F32),
                        pltpu.VMEM((1, inner), F32)],
        compiler_params=pltpu.CompilerParams(dimension_semantics=("arbitrary", "arbitrary"),
                                             vmem_limit_bytes=_vmem_limit(est)),
        name="ssd_scan",
    )(xbc, z, dt, dtt, cw, cb, dtb, dtbt, alog, alogt, dsk, ng)


def _ffn_weights(w_in, w_out):
    n_l, d, f2 = w_in.shape
    f = f2 // 2
    ck = V7X_MXU_DIM if f % V7X_MXU_DIM == 0 else V7X_LANES
    n = f // ck
    win = w_in.astype(BF16).reshape(n_l, d, 2, n, ck).transpose(0, 2, 3, 1, 4).reshape(n_l, 2 * n, d, ck)
    wout = w_out.astype(BF16).reshape(n_l, n, ck, d)
    return win, wout


def _dsa_layer(xt, bsz, seq, w_in, kvg, w_uk, w_uv, w_out, g, b, alpha):
    d = xt.shape[1]
    o1 = ATTN_HEADS * ATTN_HEAD_DIM
    o2 = o1 + KV_RANK
    o3 = o2 + IDX_HEADS * IDX_HEAD_DIM
    wb = w_in.astype(BF16)
    pad = V7X_LANES - IDX_HEAD_DIM - IDX_HEADS
    wkw = jnp.pad(wb[:, o3:], ((0, 0), (0, pad)))
    q, ckv, qi, ki, wi = _attn_inproj(xt, wb[:, :o1], wb[:, o1:o2], wb[:, o2:o3], wkw, kvg.reshape(1, -1))
    per_seq = lambda v: v.reshape(bsz, seq, -1)
    per_seq_t = lambda v: jnp.swapaxes(per_seq(v), 1, 2)
    out = _dsa_attn(per_seq(xt), per_seq_t(q), per_seq_t(qi), per_seq_t(wi), per_seq(ckv), per_seq_t(ckv),
                    per_seq(ki), jnp.swapaxes(w_uk, 1, 2).astype(BF16), jnp.swapaxes(w_uv, 1, 2).astype(BF16),
                    w_out.astype(BF16), g, b, alpha)
    return out.reshape(bsz * seq, d)


def _ssd_layer(xt, bsz, seq, w_in, conv_w, conv_b, dt_bias, a_log, d_skip, norm_g, w_out, g, b, alpha):
    inner = norm_g.shape[0]
    heads = dt_bias.shape[0]
    conv_dim = conv_w.shape[1]
    lanes = V7X_LANES * pl.cdiv(heads, V7X_LANES)
    wb = w_in.astype(BF16)
    wdt = jnp.pad(wb[:, inner + conv_dim:], ((0, 0), (0, lanes - heads)))
    z, xbc, dt = _ssd_inproj(xt, wb[:, :inner], wb[:, inner:inner + conv_dim], wdt)
    dt = dt.reshape(bsz, seq, lanes)
    padh = lambda v: jnp.pad(v, (0, lanes - heads))
    dtb, alog = padh(dt_bias), padh(a_log)
    yz = _ssd_scan(xbc.reshape(bsz, seq, conv_dim), z.reshape(bsz, seq, inner), dt, jnp.swapaxes(dt, 1, 2),
                   conv_w, conv_b.reshape(1, -1), dtb.reshape(1, -1), dtb.reshape(-1, 1),
                   alog.reshape(1, -1), alog.reshape(-1, 1),
                   jnp.repeat(d_skip, SSM_HEAD_DIM).reshape(1, -1), norm_g.reshape(1, -1), heads)
    return _proj_ln(xt, yz.reshape(bsz * seq, inner), w_out.astype(BF16), g, b, alpha)


def kernel(x, ln_g, ln_b, ffn1_w_in, ffn1_w_out, ffn2_w_in, ffn2_w_out, attn_w_in, attn_kv_norm, attn_w_uk,
           attn_w_uv, attn_w_out, ssm_w_in, ssm_conv_w, ssm_conv_b, ssm_dt_bias, ssm_a_log, ssm_d,
           ssm_norm_g, ssm_w_out):
    bsz, seq, d = x.shape
    depth = ln_g.shape[0]
    alpha = (2.0 * depth) ** 0.25
    f1_in, f1_out = _ffn_weights(ffn1_w_in, ffn1_w_out)
    f2_in, f2_out = _ffn_weights(ffn2_w_in, ffn2_w_out)
    ln = lambda i, k: (ln_g[i, k].reshape(1, d), ln_b[i, k].reshape(1, d))
    xt = x.reshape(bsz * seq, d)
    for i in range(depth):
        xt = _ffn_ln(xt, f1_in[i], f1_out[i], *ln(i, 0), alpha)
        j = i // 2
        if i % 2 == 0:
            xt = _dsa_layer(xt, bsz, seq, attn_w_in[j], attn_kv_norm[j], attn_w_uk[j], attn_w_uv[j],
                            attn_w_out[j], *ln(i, 1), alpha)
        else:
            xt = _ssd_layer(xt, bsz, seq, ssm_w_in[j], ssm_conv_w[j], ssm_conv_b[j], ssm_dt_bias[j],
                            ssm_a_log[j], ssm_d[j], ssm_norm_g[j], ssm_w_out[j], *ln(i, 1), alpha)
        xt = _ffn_ln(xt, f2_in[i], f2_out[i], *ln(i, 2), alpha)
    return xt.reshape(bsz, seq, d)
```

```python
import functools
import math

import jax
import jax.numpy as jnp
from jax import lax
from jax.experimental import pallas as pl
from jax.experimental.pallas import tpu as pltpu

F32 = jnp.float32
BF16 = jnp.bfloat16
I32 = jnp.int32
I16 = jnp.int16

ATTN_HEADS = 16
ATTN_HEAD_DIM = 64
KV_RANK = 256
IDX_HEADS = 8
IDX_HEAD_DIM = 64
TOPK_MAX = 256
SSM_HEAD_DIM = 64
SSM_GROUPS = 4
SSM_STATE = 128
SSM_CONV = 4
LN_EPS = 1e-5
RMS_EPS = 1e-6

V7X_LANES = 128
V7X_SUBLANES = 8
V7X_MXU_DIM = 256
V7X_VMEM_BYTES = 64 * 1024 * 1024

ROW_TILE = 512
Q_TILE = 128
IDX_KEY_TILE = 512
ATTN_KEY_TILE = 2 * V7X_MXU_DIM
COUNT_ROWS = 64
COUNT_ROWS16 = 128
SSD_CHUNK = 128
MASKED_LOGIT = -1e30
LOG2_E = math.log2(math.e)
INT_MIN = -(2 ** 31)
NEG_INF_KEY = INT_MIN + 0x007FFFFF


def _vmem_limit(nbytes):
    return int(min(nbytes, V7X_VMEM_BYTES - 8 * 1024 * 1024))


def _resident(shape):
    nd = len(shape)
    return pl.BlockSpec(shape, lambda *_: (0,) * nd, pipeline_mode=pl.Buffered(1))


def _layer_norm(y, g, b):
    mu = jnp.mean(y, axis=-1, keepdims=True)
    d = y - mu
    var = jnp.mean(d * d, axis=-1, keepdims=True)
    return d * lax.rsqrt(var + LN_EPS) * g + b


def _silu(v):
    return v * jax.nn.sigmoid(v)


def _ffn_ln_kernel(x_ref, win_ref, wout_ref, g_ref, b_ref, o_ref, acc_ref, *, n_chunk, alpha):
    x = x_ref[...]
    xb = x.astype(BF16)
    for c in range(n_chunk):
        gate = jnp.dot(xb, win_ref[c], preferred_element_type=F32)
        up = jnp.dot(xb, win_ref[n_chunk + c], preferred_element_type=F32)
        act = (_silu(gate) * up).astype(BF16)
        part = jnp.dot(act, wout_ref[c], preferred_element_type=F32)
        if c == 0:
            acc_ref[...] = part
        else:
            acc_ref[...] += part
    y = alpha * x + 0.5 * acc_ref[...]
    o_ref[...] = _layer_norm(y, g_ref[...], b_ref[...])


def _ffn_ln(x, win, wout, g, b, alpha):
    t, d = x.shape
    n2, _, ck = win.shape
    n_chunk = n2 // 2
    tm = min(ROW_TILE, t)
    est = (win.size + wout.size) * 2 + 5 * tm * d * 4 + 6 * tm * ck * 4 + (8 << 20)
    return pl.pallas_call(
        functools.partial(_ffn_ln_kernel, n_chunk=n_chunk, alpha=alpha),
        grid=(t // tm,),
        in_specs=[pl.BlockSpec((tm, d), lambda i: (i, 0)),
                  _resident(win.shape), _resident(wout.shape),
                  _resident(g.shape), _resident(b.shape)],
        out_specs=pl.BlockSpec((tm, d), lambda i: (i, 0)),
        out_shape=jax.ShapeDtypeStruct((t, d), F32),
        scratch_shapes=[pltpu.VMEM((tm, d), F32)],
        compiler_params=pltpu.CompilerParams(dimension_semantics=("arbitrary",),
                                             vmem_limit_bytes=_vmem_limit(est)),
        name="ffn_ln",
    )(x, win, wout, g, b)


def _proj_ln_kernel(x_ref, a_ref, w_ref, g_ref, b_ref, o_ref, *, alpha):
    m = jnp.dot(a_ref[...], w_ref[...], preferred_element_type=F32)
    o_ref[...] = _layer_norm(alpha * x_ref[...] + m, g_ref[...], b_ref[...])


def _proj_ln(x, a, w, g, b, alpha):
    t, d = x.shape
    k = a.shape[1]
    tm = min(ROW_TILE, t)
    est = w.size * 2 + 6 * tm * d * 4 + 2 * tm * k * 2 + (8 << 20)
    return pl.pallas_call(
        functools.partial(_proj_ln_kernel, alpha=alpha),
        grid=(t // tm,),
        in_specs=[pl.BlockSpec((tm, d), lambda i: (i, 0)),
                  pl.BlockSpec((tm, k), lambda i: (i, 0)),
                  _resident(w.shape), _resident(g.shape), _resident(b.shape)],
        out_specs=pl.BlockSpec((tm, d), lambda i: (i, 0)),
        out_shape=jax.ShapeDtypeStruct((t, d), F32),
        compiler_params=pltpu.CompilerParams(dimension_semantics=("arbitrary",),
                                             vmem_limit_bytes=_vmem_limit(est)),
        name="proj_ln",
    )(x, a, w, g, b)


def _attn_inproj_kernel(x_ref, wq_ref, wc_ref, wqi_ref, wkw_ref, kvg_ref,
                        q_ref, c_ref, qi_ref, ki_ref, wi_ref, *, w_scale):
    xb = x_ref[...].astype(BF16)
    q_ref[...] = jnp.dot(xb, wq_ref[...], preferred_element_type=F32).astype(BF16)
    c = jnp.dot(xb, wc_ref[...], preferred_element_type=F32)
    ms = jnp.mean(c * c, axis=-1, keepdims=True)
    c_ref[...] = (c * lax.rsqrt(ms + RMS_EPS) * kvg_ref[...]).astype(BF16)
    qi_ref[...] = jnp.dot(xb, wqi_ref[...], preferred_element_type=F32).astype(BF16)
    kw = jnp.dot(xb, wkw_ref[...], preferred_element_type=F32)
    ki_ref[...] = kw[:, :IDX_HEAD_DIM].astype(BF16)
    wi_ref[...] = kw[:, IDX_HEAD_DIM:IDX_HEAD_DIM + IDX_HEADS] * (IDX_HEADS ** -0.5) * w_scale


def _attn_inproj(x, wq, wc, wqi, wkw, kvg):
    t, d = x.shape
    tm = min(ROW_TILE, t)
    nq, nc, nqi = wq.shape[1], wc.shape[1], wqi.shape[1]
    est = (wq.size + wc.size + wqi.size + wkw.size) * 2 + 2 * tm * d * 4 + 8 * tm * (nq + nc + nqi) + (8 << 20)
    row = lambda n: pl.BlockSpec((tm, n), lambda i: (i, 0))
    return pl.pallas_call(
        functools.partial(_attn_inproj_kernel, w_scale=IDX_HEAD_DIM ** -0.5),
        grid=(t // tm,),
        in_specs=[row(d), _resident(wq.shape), _resident(wc.shape), _resident(wqi.shape),
                  _resident(wkw.shape), _resident(kvg.shape)],
        out_specs=[row(nq), row(nc), row(nqi), row(IDX_HEAD_DIM), row(IDX_HEADS)],
        out_shape=[jax.ShapeDtypeStruct((t, nq), BF16), jax.ShapeDtypeStruct((t, nc), BF16),
                   jax.ShapeDtypeStruct((t, nqi), BF16), jax.ShapeDtypeStruct((t, IDX_HEAD_DIM), BF16),
                   jax.ShapeDtypeStruct((t, IDX_HEADS), F32)],
        compiler_params=pltpu.CompilerParams(dimension_semantics=("arbitrary",),
                                             vmem_limit_bytes=_vmem_limit(est)),
        name="attn_inproj",
    )(x, wq, wc, wqi, wkw, kvg)


def _float_key(v):
    bits = pltpu.bitcast(v, I32)
    return bits ^ ((bits >> 31) & 0x7FFFFFFF)


def _store_keys(key_scr, hi_scr, off, score, causal):
    keys = _float_key(jnp.where(causal, score, -jnp.inf))
    key_scr[pl.ds(off, score.shape[0]), :] = keys
    hi_scr[pl.ds(off, score.shape[0]), :] = (keys >> 16).astype(I16)


def _topk_threshold(key_scr, hi_scr, lo_scr, thr_scr, lim_scr, n_blk, n_causal, *, topk, seq):
    tki = IDX_KEY_TILE
    tq = key_scr.shape[1]
    kf = float(topk)

    def blocks(body, init):
        return lax.fori_loop(0, n_blk, lambda kb, c: body(pl.multiple_of(kb * tki, tki), c), init)

    def count(pred):
        def body(off, c):
            keys = key_scr[pl.ds(off, tki), :]
            k_pos = off + lax.broadcasted_iota(I32, (tki, tq), 0)
            hit = jnp.where(pred(keys, k_pos), 1.0, 0.0)
            return c + jnp.sum(hit.reshape(tki // COUNT_ROWS, COUNT_ROWS, tq), axis=0)
        return jnp.sum(blocks(body, jnp.zeros((COUNT_ROWS, tq), F32)), axis=0, keepdims=True)

    def count16(ref, pred):
        def body(off, c):
            hit = jnp.where(pred(ref[pl.ds(off, tki), :]), jnp.int16(1), jnp.int16(0))
            for r in range(tki // COUNT_ROWS16):
                c = c + hit[r * COUNT_ROWS16:(r + 1) * COUNT_ROWS16]
            return c
        c = blocks(body, jnp.zeros((COUNT_ROWS16, tq), I16))
        return jnp.sum(c.astype(I32).astype(F32), axis=0, keepdims=True)

    def msb_first(ref, want, thr, c_thr, n_bits):
        def body(j, st):
            thr, c_thr = st
            cand = thr + jnp.left_shift(jnp.int32(1), n_bits - 1 - j)
            cand16 = cand.astype(I16)
            c = count16(ref, lambda v: v >= cand16)
            ok = c >= want
            return jnp.where(ok, cand, thr), jnp.where(ok, c, c_thr)
        return lax.fori_loop(0, n_bits, body, (thr, c_thr))

    assert seq < 2 ** 15
    take_all = n_causal <= kf
    i16_min = jnp.full((1, tq), -(2 ** 15), I32)
    n_all = jnp.full((1, tq), n_blk * tki).astype(F32)

    thr_hi, c_hi = msb_first(hi_scr, kf, i16_min, n_all, 16)
    thr_hi16 = thr_hi.astype(I16)
    c_above = count16(hi_scr, lambda v: v > thr_hi16)

    def park(off, c):
        lo = ((key_scr[pl.ds(off, tki), :] & 0xFFFF) - 2 ** 15).astype(I16)
        lo_scr[pl.ds(off, tki), :] = jnp.where(hi_scr[pl.ds(off, tki), :] == thr_hi16, lo, jnp.int16(-(2 ** 15)))
        return c
    blocks(park, 0)
    thr_lo, c_lo = msb_first(lo_scr, kf - c_above, i16_min, c_hi - c_above, 16)

    thr = jnp.left_shift(thr_hi, 16) + (thr_lo + 2 ** 15)
    c_thr = c_above + c_lo
    thr = jnp.maximum(thr, NEG_INF_KEY + 1)
    tied = (c_thr > kf) & ~take_all
    thr_scr[...] = thr
    lim_scr[...] = jnp.full((1, tq), seq, I32)

    @pl.when(jnp.max(jnp.where(tied, 1.0, 0.0)) > 0.5)
    def _():
        n_bits = max(1, (seq - 1).bit_length())
        need = kf - count(lambda k, p: k > thr)

        def lim_body(j, lim):
            cand = lim + jnp.left_shift(jnp.int32(1), n_bits - 1 - j)
            return jnp.where(count(lambda k, p: (k == thr) & (p < cand)) < need, cand, lim)

        lim = lax.fori_loop(0, n_bits, lim_body, jnp.zeros((1, tq), I32))
        lim_scr[...] = jnp.where(tied, lim, seq)


def _dsa_kernel(x_ref, qt_ref, qit_ref, wit_ref, ckv_ref, ckvt_ref, ki_ref, wukt_ref, wuvt_ref, wo_ref,
                g_ref, b_ref, o_ref,
                key_scr, hi_scr, lo_scr, qabs_scr, qicat_scr, acc_scr, m_scr, l_scr, ot_scr, thr_scr, lim_scr,
                *, topk, seq, alpha):
    i = pl.program_id(1)
    tq, h_n, dh, r_n = Q_TILE, ATTN_HEADS, ATTN_HEAD_DIM, KV_RANK
    hi_n, di = IDX_HEADS, IDX_HEAD_DIM
    tki, tk = IDX_KEY_TILE, ATTN_KEY_TILE
    pair = 2 * tq
    q0 = i * tq
    n_idx_blk = q0 // tki + 1
    n_kv_blk = q0 // tk + 1

    for h in range(h_n):
        qa = jnp.dot(wukt_ref[h], qt_ref[0, h * dh:(h + 1) * dh, :], preferred_element_type=F32)
        qabs_scr[:, h * tq:(h + 1) * tq] = (qa * (dh ** -0.5 * LOG2_E)).astype(BF16)
    for h in range(hi_n):
        qicat_scr[:, h * tq:(h + 1) * tq] = qit_ref[0, h * di:(h + 1) * di, :]

    wit = wit_ref[0]
    q_pos = q0 + lax.broadcasted_iota(I32, (1, tq), 1)

    def score_body(kb, carry):
        off = pl.multiple_of(kb * tki, tki)
        ki = ki_ref[0, pl.ds(off, tki), :]
        score = jnp.zeros((tki, tq), F32)
        for j in range(hi_n // 2):
            s = jnp.dot(ki, qicat_scr[:, j * pair:(j + 1) * pair], preferred_element_type=F32)
            for e in range(2):
                h = 2 * j + e
                score = score + jnp.maximum(s[:, e * tq:(e + 1) * tq], 0.0) * wit[h:h + 1, :]
        k_pos = off + lax.broadcasted_iota(I32, (tki, tq), 0)
        _store_keys(key_scr, hi_scr, off, score, k_pos <= q_pos)
        return carry

    lax.fori_loop(0, n_idx_blk, score_body, 0)
    _topk_threshold(key_scr, hi_scr, lo_scr, thr_scr, lim_scr, n_idx_blk, (q_pos + 1).astype(F32),
                    topk=topk, seq=seq)
    thr = thr_scr[...]
    lim = lim_scr[...]

    m_scr[...] = jnp.full(m_scr.shape, MASKED_LOGIT, F32)
    l_scr[...] = jnp.zeros(l_scr.shape, F32)
    acc_scr[...] = jnp.zeros(acc_scr.shape, F32)

    def attn_body(kb, carry):
        off = pl.multiple_of(kb * tk, tk)
        keys = key_scr[pl.ds(off, tk), :]
        k_pos = off + lax.broadcasted_iota(I32, (tk, tq), 0)
        sel = (keys >= thr) & ((keys > thr) | (k_pos <= lim))
        bias = jnp.where(sel, 0.0, MASKED_LOGIT)
        bias = jnp.concatenate([bias, bias], axis=1)
        c_blk = ckv_ref[0, pl.ds(off, tk), :]
        ct_blk = ckvt_ref[0, :, pl.ds(off, tk)]
        for j in range(h_n // 2):
            cols = slice(j * pair, (j + 1) * pair)
            lg = jnp.dot(c_blk, qabs_scr[:, cols], preferred_element_type=F32) + bias
            m_old = m_scr[:, cols]
            m_new = jnp.maximum(m_old, jnp.max(lg, axis=0, keepdims=True))
            a = jnp.exp2(m_old - m_new)
            p = jnp.exp2(lg - m_new)
            l_scr[:, cols] = a * l_scr[:, cols] + jnp.sum(p, axis=0, keepdims=True)
            m_scr[:, cols] = m_new
            pv = jnp.dot(ct_blk, p.astype(BF16), preferred_element_type=F32)
            acc_scr[:, cols] = acc_scr[:, cols] * a + pv
        return carry

    lax.fori_loop(0, n_kv_blk, attn_body, 0)

    for h in range(h_n):
        cols = slice(h * tq, (h + 1) * tq)
        o_lat = (acc_scr[:, cols] / l_scr[:, cols]).astype(BF16)
        ot_scr[h * dh:(h + 1) * dh, :] = jnp.dot(wuvt_ref[h], o_lat, preferred_element_type=F32)
    o = ot_scr[...].T.astype(BF16)
    mix = jnp.dot(o, wo_ref[...], preferred_element_type=F32)
    o_ref[0] = _layer_norm(alpha * x_ref[0] + mix, g_ref[...], b_ref[...])


def _dsa_attn(x, qt, qit, wit, ckv, ckvt, ki, wukt, wuvt, wo, g, b, alpha):
    bsz, seq, d = x.shape
    tq = Q_TILE
    topk = min(TOPK_MAX, seq // 4)
    h_n, r_n = ATTN_HEADS, KV_RANK
    qblk = lambda n: pl.BlockSpec((1, tq, n), lambda bi, i: (bi, i, 0))
    qblk_t = lambda n: pl.BlockSpec((1, n, tq), lambda bi, i: (bi, 0, i))
    per_batch = lambda shp: pl.BlockSpec((1,) + shp, lambda bi, i: (bi, 0, 0), pipeline_mode=pl.Buffered(1))
    est = (2 * seq * r_n * 2 + seq * V7X_LANES * 2 + tq * seq * 8
           + (wukt.size + wuvt.size + wo.size) * 2
           + h_n * tq * r_n * 6 + 2 * IDX_KEY_TILE * 2 * tq * 4 * 2 + 6 * ATTN_KEY_TILE * 2 * tq * 4
           + 2 * h_n * ATTN_HEAD_DIM * tq * (2 + 4) + 6 * tq * d * 4 + (8 << 20))
    return pl.pallas_call(
        functools.partial(_dsa_kernel, topk=topk, seq=seq, alpha=alpha),
        grid=(bsz, seq // tq),
        in_specs=[qblk(d), qblk_t(qt.shape[1]), qblk_t(qit.shape[1]), qblk_t(wit.shape[1]),
                  per_batch((seq, r_n)), per_batch((r_n, seq)), per_batch((seq, IDX_HEAD_DIM)),
                  _resident(wukt.shape), _resident(wuvt.shape), _resident(wo.shape),
                  _resident(g.shape), _resident(b.shape)],
        out_specs=qblk(d),
        out_shape=jax.ShapeDtypeStruct((bsz, seq, d), F32),
        scratch_shapes=[pltpu.VMEM((seq, tq), I32),
                        pltpu.VMEM((seq, tq), I16),
                        pltpu.VMEM((seq, tq), I16),
                        pltpu.VMEM((r_n, h_n * tq), BF16),
                        pltpu.VMEM((IDX_HEAD_DIM, IDX_HEADS * tq), BF16),
                        pltpu.VMEM((r_n, h_n * tq), F32),
                        pltpu.VMEM((1, h_n * tq), F32),
                        pltpu.VMEM((1, h_n * tq), F32),
                        pltpu.VMEM((h_n * ATTN_HEAD_DIM, tq), F32),
                        pltpu.VMEM((1, tq), I32),
                        pltpu.VMEM((1, tq), I32)],
        compiler_params=pltpu.CompilerParams(dimension_semantics=("arbitrary", "arbitrary"),
                                             vmem_limit_bytes=_vmem_limit(est)),
        name="dsa_attn",
    )(x, qt, qit, wit, ckv, ckvt, ki, wukt, wuvt, wo, g, b)


def _ssd_inproj_kernel(x_ref, wz_ref, wx_ref, wdt_ref, z_ref, xbc_ref, dt_ref):
    xb = x_ref[...].astype(BF16)
    z_ref[...] = jnp.dot(xb, wz_ref[...], preferred_element_type=F32)
    xbc_ref[...] = jnp.dot(xb, wx_ref[...], preferred_element_type=F32)
    dt_ref[...] = jnp.dot(xb, wdt_ref[...], preferred_element_type=F32)


def _ssd_inproj(x, wz, wx, wdt):
    t, d = x.shape
    tm = min(ROW_TILE, t)
    nz, nx, nd = wz.shape[1], wx.shape[1], wdt.shape[1]
    est = (wz.size + wx.size + wdt.size) * 2 + 2 * tm * d * 4 + 3 * tm * (nz + nx + nd) * 4 + (8 << 20)
    row = lambda n: pl.BlockSpec((tm, n), lambda i: (i, 0))
    return pl.pallas_call(
        _ssd_inproj_kernel,
        grid=(t // tm,),
        in_specs=[row(d), _resident(wz.shape), _resident(wx.shape), _resident(wdt.shape)],
        out_specs=[row(nz), row(nx), row(nd)],
        out_shape=[jax.ShapeDtypeStruct((t, nz), F32), jax.ShapeDtypeStruct((t, nx), F32),
                   jax.ShapeDtypeStruct((t, nd), F32)],
        compiler_params=pltpu.CompilerParams(dimension_semantics=("arbitrary",),
                                             vmem_limit_bytes=_vmem_limit(est)),
        name="ssd_inproj",
    )(x, wz, wx, wdt)


def _cumsum(a, axis):
    n = a.shape[axis]
    idx = lax.broadcasted_iota(I32, a.shape, axis)
    s = 1
    while s < n:
        a = a + jnp.where(idx >= s, pltpu.roll(a, s, axis), 0.0)
        s *= 2
    return a


def _softplus(v):
    return jnp.maximum(v, 0.0) + jnp.log1p(jnp.exp(-jnp.abs(v)))


def _ssd_scan_kernel(xbc_ref, z_ref, dt_ref, dtt_ref, cw_ref, cb_ref, dtb_ref, dtbt_ref, alog_ref, alogt_ref,
                     dsk_ref, ng_ref, o_ref,
                     win_scr, state_scr, xdte_scr, y_scr, cd_scr, *, inner, heads):
    c_idx = pl.program_id(1)
    ln = SSD_CHUNK
    g_n, n_st, p_dim = SSM_GROUPS, SSM_STATE, SSM_HEAD_DIM
    gw = inner // g_n
    pairs_per_group = gw // V7X_LANES
    tail = V7X_SUBLANES

    @pl.when(c_idx == 0)
    def _():
        win_scr[0:tail, :] = jnp.zeros((tail, win_scr.shape[1]), F32)
        state_scr[...] = jnp.zeros(state_scr.shape, F32)

    xbc = xbc_ref[0]
    win_scr[tail:tail + ln, :] = xbc
    conv = cb_ref[...] + cw_ref[0:1, :] * win_scr[tail - SSM_CONV + 1:tail - SSM_CONV + 1 + ln, :]
    for k in range(1, SSM_CONV):
        lo = tail - SSM_CONV + 1 + k
        conv = conv + cw_ref[k:k + 1, :] * win_scr[lo:lo + ln, :]
    win_scr[0:tail, :] = xbc[ln - tail:ln, :]
    act = _silu(conv)
    xs = act[:, :inner]
    bm = act[:, inner:inner + g_n * n_st]
    cm = act[:, inner + g_n * n_st:]

    a_row = -jnp.exp(alog_ref[...])
    a_col = -jnp.exp(alogt_ref[...])
    dt = _softplus(dt_ref[0] + dtb_ref[...])
    dtt = _softplus(dtt_ref[0] + dtbt_ref[...])
    acs = _cumsum(dt * a_row, 0)
    acst = _cumsum(dtt * a_col, 1)
    e_acs = jnp.exp(acs)
    last = acs[ln - 1:ln, :]
    e_end = jnp.exp(last - acs)
    e_last = jnp.exp(last)

    tri = lax.broadcasted_iota(I32, (ln, ln), 0) >= lax.broadcasted_iota(I32, (ln, ln), 1)
    low = lax.broadcasted_iota(I32, (ln, V7X_LANES), 1) < p_dim
    low1 = low[0:1, :]

    def pair_cols(v, h0):
        return jnp.where(low, v[:, h0:h0 + 1], v[:, h0 + 1:h0 + 2])

    for g in range(g_n):
        bm_g = bm[:, g * n_st:(g + 1) * n_st]
        cm_g = cm[:, g * n_st:(g + 1) * n_st].astype(BF16)
        bmt_g = bm_g.T.astype(BF16)
        cb = jnp.dot(cm_g, bmt_g, preferred_element_type=F32)
        y_off = jnp.dot(cm_g, state_scr[g].astype(BF16), preferred_element_type=F32)
        for jp in range(pairs_per_group):
            j = g * pairs_per_group + jp
            h0 = 2 * j
            lanes = slice(j * V7X_LANES, (j + 1) * V7X_LANES)
            d0 = jnp.where(tri, jnp.exp(acs[:, h0:h0 + 1] - acst[h0:h0 + 1, :]), 0.0)
            d1 = jnp.where(tri, jnp.exp(acs[:, h0 + 1:h0 + 2] - acst[h0 + 1:h0 + 2, :]), 0.0)
            m_pair = jnp.concatenate([(cb * d0).astype(BF16), (cb * d1).astype(BF16)], axis=1)
            x_pair = xs[:, lanes]
            xdt = x_pair * pair_cols(dt, h0)
            xb = xdt.astype(BF16)
            zero = jnp.zeros_like(xb)
            rhs = jnp.concatenate([jnp.where(low, xb, zero), jnp.where(low, zero, xb)], axis=0)
            y_diag = jnp.dot(m_pair, rhs, preferred_element_type=F32)
            y = y_diag + y_off[:, jp * V7X_LANES:(jp + 1) * V7X_LANES] * pair_cols(e_acs, h0)
            y_scr[:, lanes] = y + x_pair * dsk_ref[:, lanes]
            xdte_scr[:, lanes] = (xdt * pair_cols(e_end, h0)).astype(BF16)
            cd_scr[:, lanes] = jnp.where(low1, e_last[:, h0:h0 + 1], e_last[:, h0 + 1:h0 + 2])
        new = jnp.dot(bmt_g, xdte_scr[:, g * gw:(g + 1) * gw], preferred_element_type=F32)
        state_scr[g] = state_scr[g] * cd_scr[:, g * gw:(g + 1) * gw] + new

    yz = y_scr[...] * _silu(z_ref[0])
    for g in range(g_n):
        blk = yz[:, g * gw:(g + 1) * gw]
        ms = jnp.mean(blk * blk, axis=-1, keepdims=True)
        o_ref[0, :, g * gw:(g + 1) * gw] = (blk * lax.rsqrt(ms + RMS_EPS)
                                            * ng_ref[:, g * gw:(g + 1) * gw]).astype(BF16)


def _ssd_scan(xbc, z, dt, dtt, cw, cb, dtb, dtbt, alog, alogt, dsk, ng, heads):
    bsz, seq, conv_dim = xbc.shape
    inner = z.shape[2]
    ln = SSD_CHUNK
    lanes = dt.shape[2]
    gw = inner // SSM_GROUPS
    blk = lambda n: pl.BlockSpec((1, ln, n), lambda bi, c: (bi, c, 0))
    est = (2 * ln * (conv_dim + inner) * 4 + 2 * ln * inner * 2 + (ln + 8) * conv_dim * 4
           + SSM_GROUPS * SSM_STATE * gw * 4 + ln * inner * 6 + 12 * ln * conv_dim * 4 + (8 << 20))
    return pl.pallas_call(
        functools.partial(_ssd_scan_kernel, inner=inner, heads=heads),
        grid=(bsz, seq // ln),
        in_specs=[blk(conv_dim), blk(inner), blk(lanes),
                  pl.BlockSpec((1, lanes, ln), lambda bi, c: (bi, 0, c)),
                  _resident(cw.shape), _resident(cb.shape), _resident(dtb.shape), _resident(dtbt.shape),
                  _resident(alog.shape), _resident(alogt.shape), _resident(dsk.shape), _resident(ng.shape)],
        out_specs=blk(inner),
        out_shape=jax.ShapeDtypeStruct((bsz, seq, inner), BF16),
        scratch_shapes=[pltpu.VMEM((ln + V7X_SUBLANES, conv_dim), F32),
                        pltpu.VMEM((SSM_GROUPS, SSM_STATE, gw), F32),
                        pltpu.VMEM((ln, inner), BF16),
                        pltpu.VMEM((ln, inner), F32),
                        pltpu.VMEM((1, inner), F32)],
        compiler_params=pltpu.CompilerParams(dimension_semantics=("arbitrary", "arbitrary"),
                                             vmem_limit_bytes=_vmem_limit(est)),
        name="ssd_scan",
    )(xbc, z, dt, dtt, cw, cb, dtb, dtbt, alog, alogt, dsk, ng)


def _ffn_weights(w_in, w_out):
    n_l, d, f2 = w_in.shape
    f = f2 // 2
    ck = V7X_MXU_DIM if f % V7X_MXU_DIM == 0 else V7X_LANES
    n = f // ck
    win = w_in.astype(BF16).reshape(n_l, d, 2, n, ck).transpose(0, 2, 3, 1, 4).reshape(n_l, 2 * n, d, ck)
    wout = w_out.astype(BF16).reshape(n_l, n, ck, d)
    return win, wout


def _dsa_layer(xt, bsz, seq, w_in, kvg, w_uk, w_uv, w_out, g, b, alpha):
    d = xt.shape[1]
    o1 = ATTN_HEADS * ATTN_HEAD_DIM
    o2 = o1 + KV_RANK
    o3 = o2 + IDX_HEADS * IDX_HEAD_DIM
    wb = w_in.astype(BF16)
    pad = V7X_LANES - IDX_HEAD_DIM - IDX_HEADS
    wkw = jnp.pad(wb[:, o3:], ((0, 0), (0, pad)))
    q, ckv, qi, ki, wi = _attn_inproj(xt, wb[:, :o1], wb[:, o1:o2], wb[:, o2:o3], wkw, kvg.reshape(1, -1))
    per_seq = lambda v: v.reshape(bsz, seq, -1)
    per_seq_t = lambda v: jnp.swapaxes(per_seq(v), 1, 2)
    out = _dsa_attn(per_seq(xt), per_seq_t(q), per_seq_t(qi), per_seq_t(wi), per_seq(ckv), per_seq_t(ckv),
                    per_seq(ki), jnp.swapaxes(w_uk, 1, 2).astype(BF16), jnp.swapaxes(w_uv, 1, 2).astype(BF16),
                    w_out.astype(BF16), g, b, alpha)
    return out.reshape(bsz * seq, d)


def _ssd_layer(xt, bsz, seq, w_in, conv_w, conv_b, dt_bias, a_log, d_skip, norm_g, w_out, g, b, alpha):
    inner = norm_g.shape[0]
    heads = dt_bias.shape[0]
    conv_dim = conv_w.shape[1]
    lanes = V7X_LANES * pl.cdiv(heads, V7X_LANES)
    wb = w_in.astype(BF16)
    wdt = jnp.pad(wb[:, inner + conv_dim:], ((0, 0), (0, lanes - heads)))
    z, xbc, dt = _ssd_inproj(xt, wb[:, :inner], wb[:, inner:inner + conv_dim], wdt)
    dt = dt.reshape(bsz, seq, lanes)
    padh = lambda v: jnp.pad(v, (0, lanes - heads))
    dtb, alog = padh(dt_bias), padh(a_log)
    yz = _ssd_scan(xbc.reshape(bsz, seq, conv_dim), z.reshape(bsz, seq, inner), dt, jnp.swapaxes(dt, 1, 2),
                   conv_w, conv_b.reshape(1, -1), dtb.reshape(1, -1), dtb.reshape(-1, 1),
                   alog.reshape(1, -1), alog.reshape(-1, 1),
                   jnp.repeat(d_skip, SSM_HEAD_DIM).reshape(1, -1), norm_g.reshape(1, -1), heads)
    return _proj_ln(xt, yz.reshape(bsz * seq, inner), w_out.astype(BF16), g, b, alpha)


def kernel(x, ln_g, ln_b, ffn1_w_in, ffn1_w_out, ffn2_w_in, ffn2_w_out, attn_w_in, attn_kv_norm, attn_w_uk,
           attn_w_uv, attn_w_out, ssm_w_in, ssm_conv_w, ssm_conv_b, ssm_dt_bias, ssm_a_log, ssm_d,
           ssm_norm_g, ssm_w_out):
    bsz, seq, d = x.shape
    depth = ln_g.shape[0]
    alpha = (2.0 * depth) ** 0.25
    f1_in, f1_out = _ffn_weights(ffn1_w_in, ffn1_w_out)
    f2_in, f2_out = _ffn_weights(ffn2_w_in, ffn2_w_out)
    ln = lambda i, k: (ln_g[i, k].reshape(1, d), ln_b[i, k].reshape(1, d))
    xt = x.reshape(bsz * seq, d)
    for i in range(depth):
        xt = _ffn_ln(xt, f1_in[i], f1_out[i], *ln(i, 0), alpha)
        j = i // 2
        if i % 2 == 0:
            xt = _dsa_layer(xt, bsz, seq, attn_w_in[j], attn_kv_norm[j], attn_w_uk[j], attn_w_uv[j],
                            attn_w_out[j], *ln(i, 1), alpha)
        else:
            xt = _ssd_layer(xt, bsz, seq, ssm_w_in[j], ssm_conv_w[j], ssm_conv_b[j], ssm_dt_bias[j],
                            ssm_a_log[j], ssm_d[j], ssm_norm_g[j], ssm_w_out[j], *ln(i, 1), alpha)
        xt = _ffn_ln(xt, f2_in[i], f2_out[i], *ln(i, 2), alpha)
    return xt.reshape(bsz, seq, d)
```

```python
import functools
import math

import jax
import jax.numpy as jnp
from jax import lax
from jax.experimental import pallas as pl
from jax.experimental.pallas import tpu as pltpu

F32 = jnp.float32
BF16 = jnp.bfloat16
I32 = jnp.int32

ATTN_HEADS = 16
ATTN_HEAD_DIM = 64
KV_RANK = 256
IDX_HEADS = 8
IDX_HEAD_DIM = 64
TOPK_MAX = 256
SSM_HEAD_DIM = 64
SSM_GROUPS = 4
SSM_STATE = 128
SSM_CONV = 4
LN_EPS = 1e-5
RMS_EPS = 1e-6

V7X_LANES = 128
V7X_SUBLANES = 8
V7X_MXU_DIM = 256
V7X_VMEM_BYTES = 64 * 1024 * 1024

ROW_TILE = 512
Q_TILE = 128
IDX_KEY_TILE = 512
ATTN_KEY_TILE = 2 * V7X_MXU_DIM
COUNT_ROWS = 64
SSD_CHUNK = 128
MASKED_LOGIT = -1e30
LOG2_E = math.log2(math.e)
INT_MIN = -(2 ** 31)
NEG_INF_KEY = INT_MIN + 0x007FFFFF


def _vmem_limit(nbytes):
    return int(min(nbytes, V7X_VMEM_BYTES - 8 * 1024 * 1024))


def _resident(shape):
    nd = len(shape)
    return pl.BlockSpec(shape, lambda *_: (0,) * nd, pipeline_mode=pl.Buffered(1))


def _layer_norm(y, g, b):
    mu = jnp.mean(y, axis=-1, keepdims=True)
    d = y - mu
    var = jnp.mean(d * d, axis=-1, keepdims=True)
    return d * lax.rsqrt(var + LN_EPS) * g + b


def _silu(v):
    return v * jax.nn.sigmoid(v)


def _ffn_ln_kernel(x_ref, win_ref, wout_ref, g_ref, b_ref, o_ref, acc_ref, *, n_chunk, alpha):
    x = x_ref[...]
    xb = x.astype(BF16)
    for c in range(n_chunk):
        gate = jnp.dot(xb, win_ref[c], preferred_element_type=F32)
        up = jnp.dot(xb, win_ref[n_chunk + c], preferred_element_type=F32)
        act = (_silu(gate) * up).astype(BF16)
        part = jnp.dot(act, wout_ref[c], preferred_element_type=F32)
        if c == 0:
            acc_ref[...] = part
        else:
            acc_ref[...] += part
    y = alpha * x + 0.5 * acc_ref[...]
    o_ref[...] = _layer_norm(y, g_ref[...], b_ref[...])


def _ffn_ln(x, win, wout, g, b, alpha):
    t, d = x.shape
    n2, _, ck = win.shape
    n_chunk = n2 // 2
    tm = min(ROW_TILE, t)
    est = (win.size + wout.size) * 2 + 5 * tm * d * 4 + 6 * tm * ck * 4 + (8 << 20)
    return pl.pallas_call(
        functools.partial(_ffn_ln_kernel, n_chunk=n_chunk, alpha=alpha),
        grid=(t // tm,),
        in_specs=[pl.BlockSpec((tm, d), lambda i: (i, 0)),
                  _resident(win.shape), _resident(wout.shape),
                  _resident(g.shape), _resident(b.shape)],
        out_specs=pl.BlockSpec((tm, d), lambda i: (i, 0)),
        out_shape=jax.ShapeDtypeStruct((t, d), F32),
        scratch_shapes=[pltpu.VMEM((tm, d), F32)],
        compiler_params=pltpu.CompilerParams(dimension_semantics=("arbitrary",),
                                             vmem_limit_bytes=_vmem_limit(est)),
        name="ffn_ln",
    )(x, win, wout, g, b)


def _proj_ln_kernel(x_ref, a_ref, w_ref, g_ref, b_ref, o_ref, *, alpha):
    m = jnp.dot(a_ref[...], w_ref[...], preferred_element_type=F32)
    o_ref[...] = _layer_norm(alpha * x_ref[...] + m, g_ref[...], b_ref[...])


def _proj_ln(x, a, w, g, b, alpha):
    t, d = x.shape
    k = a.shape[1]
    tm = min(ROW_TILE, t)
    est = w.size * 2 + 6 * tm * d * 4 + 2 * tm * k * 2 + (8 << 20)
    return pl.pallas_call(
        functools.partial(_proj_ln_kernel, alpha=alpha),
        grid=(t // tm,),
        in_specs=[pl.BlockSpec((tm, d), lambda i: (i, 0)),
                  pl.BlockSpec((tm, k), lambda i: (i, 0)),
                  _resident(w.shape), _resident(g.shape), _resident(b.shape)],
        out_specs=pl.BlockSpec((tm, d), lambda i: (i, 0)),
        out_shape=jax.ShapeDtypeStruct((t, d), F32),
        compiler_params=pltpu.CompilerParams(dimension_semantics=("arbitrary",),
                                             vmem_limit_bytes=_vmem_limit(est)),
        name="proj_ln",
    )(x, a, w, g, b)


def _attn_inproj_kernel(x_ref, wq_ref, wc_ref, wqi_ref, wkw_ref, kvg_ref,
                        q_ref, c_ref, qi_ref, ki_ref, wi_ref, *, w_scale):
    xb = x_ref[...].astype(BF16)
    q_ref[...] = jnp.dot(xb, wq_ref[...], preferred_element_type=F32).astype(BF16)
    c = jnp.dot(xb, wc_ref[...], preferred_element_type=F32)
    ms = jnp.mean(c * c, axis=-1, keepdims=True)
    c_ref[...] = (c * lax.rsqrt(ms + RMS_EPS) * kvg_ref[...]).astype(BF16)
    qi_ref[...] = jnp.dot(xb, wqi_ref[...], preferred_element_type=F32).astype(BF16)
    kw = jnp.dot(xb, wkw_ref[...], preferred_element_type=F32)
    ki_ref[...] = kw[:, :IDX_HEAD_DIM].astype(BF16)
    wi_ref[...] = kw[:, IDX_HEAD_DIM:IDX_HEAD_DIM + IDX_HEADS] * (IDX_HEADS ** -0.5) * w_scale


def _attn_inproj(x, wq, wc, wqi, wkw, kvg):
    t, d = x.shape
    tm = min(ROW_TILE, t)
    nq, nc, nqi = wq.shape[1], wc.shape[1], wqi.shape[1]
    est = (wq.size + wc.size + wqi.size + wkw.size) * 2 + 2 * tm * d * 4 + 8 * tm * (nq + nc + nqi) + (8 << 20)
    row = lambda n: pl.BlockSpec((tm, n), lambda i: (i, 0))
    return pl.pallas_call(
        functools.partial(_attn_inproj_kernel, w_scale=IDX_HEAD_DIM ** -0.5),
        grid=(t // tm,),
        in_specs=[row(d), _resident(wq.shape), _resident(wc.shape), _resident(wqi.shape),
                  _resident(wkw.shape), _resident(kvg.shape)],
        out_specs=[row(nq), row(nc), row(nqi), row(IDX_HEAD_DIM), row(IDX_HEADS)],
        out_shape=[jax.ShapeDtypeStruct((t, nq), BF16), jax.ShapeDtypeStruct((t, nc), BF16),
                   jax.ShapeDtypeStruct((t, nqi), BF16), jax.ShapeDtypeStruct((t, IDX_HEAD_DIM), BF16),
                   jax.ShapeDtypeStruct((t, IDX_HEADS), F32)],
        compiler_params=pltpu.CompilerParams(dimension_semantics=("arbitrary",),
                                             vmem_limit_bytes=_vmem_limit(est)),
        name="attn_inproj",
    )(x, wq, wc, wqi, wkw, kvg)


def _float_key(v):
    bits = pltpu.bitcast(v, I32)
    return bits ^ ((bits >> 31) & 0x7FFFFFFF)


def _store_keys(key_scr, off, score, causal):
    key_scr[pl.ds(off, score.shape[0]), :] = _float_key(jnp.where(causal, score, -jnp.inf))


def _topk_threshold(key_scr, thr_scr, lim_scr, n_blk, n_causal, *, topk, seq):
    tki = IDX_KEY_TILE
    tq = key_scr.shape[1]
    kf = float(topk)

    def count(pred):
        def body(kb, c):
            off = pl.multiple_of(kb * tki, tki)
            keys = key_scr[pl.ds(off, tki), :]
            k_pos = off + lax.broadcasted_iota(I32, (tki, tq), 0)
            hit = jnp.where(pred(keys, k_pos), 1.0, 0.0)
            return c + jnp.sum(hit.reshape(tki // COUNT_ROWS, COUNT_ROWS, tq), axis=0)
        c = lax.fori_loop(0, n_blk, body, jnp.zeros((COUNT_ROWS, tq), F32))
        return jnp.sum(c, axis=0, keepdims=True)

    def add_bit(j, st):
        thr, c_thr = st
        cand = thr + jnp.left_shift(jnp.int32(1), 30 - j)
        c = count(lambda k, p: k >= cand)
        ok = c >= kf
        return jnp.where(ok, cand, thr), jnp.where(ok, c, c_thr)

    take_all = n_causal <= kf
    c0 = count(lambda k, p: k >= 0)
    n_all = jnp.full((1, tq), n_blk * tki).astype(F32)
    thr, c_thr = lax.fori_loop(
        0, 31, add_bit, (jnp.where(c0 >= kf, 0, INT_MIN).astype(I32), jnp.where(c0 >= kf, c0, n_all)))
    thr = jnp.maximum(thr, NEG_INF_KEY + 1)
    tied = (c_thr > kf) & ~take_all
    thr_scr[...] = thr
    lim_scr[...] = jnp.full((1, tq), seq, I32)

    @pl.when(jnp.max(jnp.where(tied, 1.0, 0.0)) > 0.5)
    def _():
        n_bits = max(1, (seq - 1).bit_length())
        need = kf - count(lambda k, p: k > thr)

        def lim_body(j, lim):
            cand = lim + jnp.left_shift(jnp.int32(1), n_bits - 1 - j)
            return jnp.where(count(lambda k, p: (k == thr) & (p < cand)) < need, cand, lim)

        lim = lax.fori_loop(0, n_bits, lim_body, jnp.zeros((1, tq), I32))
        lim_scr[...] = jnp.where(tied, lim, seq)


def _dsa_kernel(x_ref, qt_ref, qit_ref, wit_ref, ckv_ref, ckvt_ref, ki_ref, wukt_ref, wuvt_ref, wo_ref,
                g_ref, b_ref, o_ref,
                key_scr, qabs_scr, qicat_scr, acc_scr, m_scr, l_scr, ot_scr, thr_scr, lim_scr,
                *, topk, seq, alpha):
    i = pl.program_id(1)
    tq, h_n, dh, r_n = Q_TILE, ATTN_HEADS, ATTN_HEAD_DIM, KV_RANK
    hi_n, di = IDX_HEADS, IDX_HEAD_DIM
    tki, tk = IDX_KEY_TILE, ATTN_KEY_TILE
    pair = 2 * tq
    q0 = i * tq
    n_idx_blk = q0 // tki + 1
    n_kv_blk = q0 // tk + 1

    for h in range(h_n):
        qa = jnp.dot(wukt_ref[h], qt_ref[0, h * dh:(h + 1) * dh, :], preferred_element_type=F32)
        qabs_scr[:, h * tq:(h + 1) * tq] = (qa * (dh ** -0.5 * LOG2_E)).astype(BF16)
    for h in range(hi_n):
        qicat_scr[:, h * tq:(h + 1) * tq] = qit_ref[0, h * di:(h + 1) * di, :]

    wit = wit_ref[0]
    q_pos = q0 + lax.broadcasted_iota(I32, (1, tq), 1)

    def score_body(kb, carry):
        off = pl.multiple_of(kb * tki, tki)
        ki = ki_ref[0, pl.ds(off, tki), :]
        score = jnp.zeros((tki, tq), F32)
        for j in range(hi_n // 2):
            s = jnp.dot(ki, qicat_scr[:, j * pair:(j + 1) * pair], preferred_element_type=F32)
            for e in range(2):
                h = 2 * j + e
                score = score + jnp.maximum(s[:, e * tq:(e + 1) * tq], 0.0) * wit[h:h + 1, :]
        k_pos = off + lax.broadcasted_iota(I32, (tki, tq), 0)
        _store_keys(key_scr, off, score, k_pos <= q_pos)
        return carry

    lax.fori_loop(0, n_idx_blk, score_body, 0)
    _topk_threshold(key_scr, thr_scr, lim_scr, n_idx_blk, (q_pos + 1).astype(F32), topk=topk, seq=seq)
    thr = thr_scr[...]
    lim = lim_scr[...]

    m_scr[...] = jnp.full(m_scr.shape, MASKED_LOGIT, F32)
    l_scr[...] = jnp.zeros(l_scr.shape, F32)
    acc_scr[...] = jnp.zeros(acc_scr.shape, F32)

    def attn_body(kb, carry):
        off = pl.multiple_of(kb * tk, tk)
        keys = key_scr[pl.ds(off, tk), :]
        k_pos = off + lax.broadcasted_iota(I32, (tk, tq), 0)
        sel = (keys >= thr) & ((keys > thr) | (k_pos <= lim))
        bias = jnp.where(sel, 0.0, MASKED_LOGIT)
        bias = jnp.concatenate([bias, bias], axis=1)
        c_blk = ckv_ref[0, pl.ds(off, tk), :]
        ct_blk = ckvt_ref[0, :, pl.ds(off, tk)]
        for j in range(h_n // 2):
            cols = slice(j * pair, (j + 1) * pair)
            lg = jnp.dot(c_blk, qabs_scr[:, cols], preferred_element_type=F32) + bias
            m_old = m_scr[:, cols]
            m_new = jnp.maximum(m_old, jnp.max(lg, axis=0, keepdims=True))
            a = jnp.exp2(m_old - m_new)
            p = jnp.exp2(lg - m_new)
            l_scr[:, cols] = a * l_scr[:, cols] + jnp.sum(p, axis=0, keepdims=True)
            m_scr[:, cols] = m_new
            pv = jnp.dot(ct_blk, p.astype(BF16), preferred_element_type=F32)
            acc_scr[:, cols] = acc_scr[:, cols] * a + pv
        return carry

    lax.fori_loop(0, n_kv_blk, attn_body, 0)

    for h in range(h_n):
        cols = slice(h * tq, (h + 1) * tq)
        o_lat = (acc_scr[:, cols] / l_scr[:, cols]).astype(BF16)
        ot_scr[h * dh:(h + 1) * dh, :] = jnp.dot(wuvt_ref[h], o_lat, preferred_element_type=F32)
    o = ot_scr[...].T.astype(BF16)
    mix = jnp.dot(o, wo_ref[...], preferred_element_type=F32)
    o_ref[0] = _layer_norm(alpha * x_ref[0] + mix, g_ref[...], b_ref[...])


def _dsa_attn(x, qt, qit, wit, ckv, ckvt, ki, wukt, wuvt, wo, g, b, alpha):
    bsz, seq, d = x.shape
    tq = Q_TILE
    topk = min(TOPK_MAX, seq // 4)
    h_n, r_n = ATTN_HEADS, KV_RANK
    qblk = lambda n: pl.BlockSpec((1, tq, n), lambda bi, i: (bi, i, 0))
    qblk_t = lambda n: pl.BlockSpec((1, n, tq), lambda bi, i: (bi, 0, i))
    per_batch = lambda shp: pl.BlockSpec((1,) + shp, lambda bi, i: (bi, 0, 0), pipeline_mode=pl.Buffered(1))
    est = (2 * seq * r_n * 2 + seq * V7X_LANES * 2 + tq * seq * 4
           + (wukt.size + wuvt.size + wo.size) * 2
           + h_n * tq * r_n * 6 + 2 * IDX_KEY_TILE * 2 * tq * 4 * 2 + 6 * ATTN_KEY_TILE * 2 * tq * 4
           + 2 * h_n * ATTN_HEAD_DIM * tq * (2 + 4) + 6 * tq * d * 4 + (8 << 20))
    return pl.pallas_call(
        functools.partial(_dsa_kernel, topk=topk, seq=seq, alpha=alpha),
        grid=(bsz, seq // tq),
        in_specs=[qblk(d), qblk_t(qt.shape[1]), qblk_t(qit.shape[1]), qblk_t(wit.shape[1]),
                  per_batch((seq, r_n)), per_batch((r_n, seq)), per_batch((seq, IDX_HEAD_DIM)),
                  _resident(wukt.shape), _resident(wuvt.shape), _resident(wo.shape),
                  _resident(g.shape), _resident(b.shape)],
        out_specs=qblk(d),
        out_shape=jax.ShapeDtypeStruct((bsz, seq, d), F32),
        scratch_shapes=[pltpu.VMEM((seq, tq), I32),
                        pltpu.VMEM((r_n, h_n * tq), BF16),
                        pltpu.VMEM((IDX_HEAD_DIM, IDX_HEADS * tq), BF16),
                        pltpu.VMEM((r_n, h_n * tq), F32),
                        pltpu.VMEM((1, h_n * tq), F32),
                        pltpu.VMEM((1, h_n * tq), F32),
                        pltpu.VMEM((h_n * ATTN_HEAD_DIM, tq), F32),
                        pltpu.VMEM((1, tq), I32),
                        pltpu.VMEM((1, tq), I32)],
        compiler_params=pltpu.CompilerParams(dimension_semantics=("arbitrary", "arbitrary"),
                                             vmem_limit_bytes=_vmem_limit(est)),
        name="dsa_attn",
    )(x, qt, qit, wit, ckv, ckvt, ki, wukt, wuvt, wo, g, b)


def _ssd_inproj_kernel(x_ref, wz_ref, wx_ref, wdt_ref, z_ref, xbc_ref, dt_ref):
    xb = x_ref[...].astype(BF16)
    z_ref[...] = jnp.dot(xb, wz_ref[...], preferred_element_type=F32)
    xbc_ref[...] = jnp.dot(xb, wx_ref[...], preferred_element_type=F32)
    dt_ref[...] = jnp.dot(xb, wdt_ref[...], preferred_element_type=F32)


def _ssd_inproj(x, wz, wx, wdt):
    t, d = x.shape
    tm = min(ROW_TILE, t)
    nz, nx, nd = wz.shape[1], wx.shape[1], wdt.shape[1]
    est = (wz.size + wx.size + wdt.size) * 2 + 2 * tm * d * 4 + 3 * tm * (nz + nx + nd) * 4 + (8 << 20)
    row = lambda n: pl.BlockSpec((tm, n), lambda i: (i, 0))
    return pl.pallas_call(
        _ssd_inproj_kernel,
        grid=(t // tm,),
        in_specs=[row(d), _resident(wz.shape), _resident(wx.shape), _resident(wdt.shape)],
        out_specs=[row(nz), row(nx), row(nd)],
        out_shape=[jax.ShapeDtypeStruct((t, nz), F32), jax.ShapeDtypeStruct((t, nx), F32),
                   jax.ShapeDtypeStruct((t, nd), F32)],
        compiler_params=pltpu.CompilerParams(dimension_semantics=("arbitrary",),
                                             vmem_limit_bytes=_vmem_limit(est)),
        name="ssd_inproj",
    )(x, wz, wx, wdt)


def _cumsum(a, axis):
    n = a.shape[axis]
    idx = lax.broadcasted_iota(I32, a.shape, axis)
    s = 1
    while s < n:
        a = a + jnp.where(idx >= s, pltpu.roll(a, s, axis), 0.0)
        s *= 2
    return a


def _softplus(v):
    return jnp.maximum(v, 0.0) + jnp.log1p(jnp.exp(-jnp.abs(v)))


def _ssd_scan_kernel(xbc_ref, z_ref, dt_ref, dtt_ref, cw_ref, cb_ref, dtb_ref, dtbt_ref, alog_ref, alogt_ref,
                     dsk_ref, ng_ref, o_ref,
                     win_scr, state_scr, xdte_scr, y_scr, cd_scr, *, inner, heads):
    c_idx = pl.program_id(1)
    ln = SSD_CHUNK
    g_n, n_st, p_dim = SSM_GROUPS, SSM_STATE, SSM_HEAD_DIM
    gw = inner // g_n
    pairs_per_group = gw // V7X_LANES
    tail = V7X_SUBLANES

    @pl.when(c_idx == 0)
    def _():
        win_scr[0:tail, :] = jnp.zeros((tail, win_scr.shape[1]), F32)
        state_scr[...] = jnp.zeros(state_scr.shape, F32)

    xbc = xbc_ref[0]
    win_scr[tail:tail + ln, :] = xbc
    conv = cb_ref[...] + cw_ref[0:1, :] * win_scr[tail - SSM_CONV + 1:tail - SSM_CONV + 1 + ln, :]
    for k in range(1, SSM_CONV):
        lo = tail - SSM_CONV + 1 + k
        conv = conv + cw_ref[k:k + 1, :] * win_scr[lo:lo + ln, :]
    win_scr[0:tail, :] = xbc[ln - tail:ln, :]
    act = _silu(conv)
    xs = act[:, :inner]
    bm = act[:, inner:inner + g_n * n_st]
    cm = act[:, inner + g_n * n_st:]

    a_row = -jnp.exp(alog_ref[...])
    a_col = -jnp.exp(alogt_ref[...])
    dt = _softplus(dt_ref[0] + dtb_ref[...])
    dtt = _softplus(dtt_ref[0] + dtbt_ref[...])
    acs = _cumsum(dt * a_row, 0)
    acst = _cumsum(dtt * a_col, 1)
    e_acs = jnp.exp(acs)
    last = acs[ln - 1:ln, :]
    e_end = jnp.exp(last - acs)
    e_last = jnp.exp(last)

    tri = lax.broadcasted_iota(I32, (ln, ln), 0) >= lax.broadcasted_iota(I32, (ln, ln), 1)
    low = lax.broadcasted_iota(I32, (ln, V7X_LANES), 1) < p_dim
    low1 = low[0:1, :]

    def pair_cols(v, h0):
        return jnp.where(low, v[:, h0:h0 + 1], v[:, h0 + 1:h0 + 2])

    for g in range(g_n):
        bm_g = bm[:, g * n_st:(g + 1) * n_st]
        cm_g = cm[:, g * n_st:(g + 1) * n_st].astype(BF16)
        bmt_g = bm_g.T.astype(BF16)
        cb = jnp.dot(cm_g, bmt_g, preferred_element_type=F32)
        y_off = jnp.dot(cm_g, state_scr[g].astype(BF16), preferred_element_type=F32)
        for jp in range(pairs_per_group):
            j = g * pairs_per_group + jp
            h0 = 2 * j
            lanes = slice(j * V7X_LANES, (j + 1) * V7X_LANES)
            d0 = jnp.where(tri, jnp.exp(acs[:, h0:h0 + 1] - acst[h0:h0 + 1, :]), 0.0)
            d1 = jnp.where(tri, jnp.exp(acs[:, h0 + 1:h0 + 2] - acst[h0 + 1:h0 + 2, :]), 0.0)
            m_pair = jnp.concatenate([(cb * d0).astype(BF16), (cb * d1).astype(BF16)], axis=1)
            x_pair = xs[:, lanes]
            xdt = x_pair * pair_cols(dt, h0)
            xb = xdt.astype(BF16)
            zero = jnp.zeros_like(xb)
            rhs = jnp.concatenate([jnp.where(low, xb, zero), jnp.where(low, zero, xb)], axis=0)
            y_diag = jnp.dot(m_pair, rhs, preferred_element_type=F32)
            y = y_diag + y_off[:, jp * V7X_LANES:(jp + 1) * V7X_LANES] * pair_cols(e_acs, h0)
            y_scr[:, lanes] = y + x_pair * dsk_ref[:, lanes]
            xdte_scr[:, lanes] = (xdt * pair_cols(e_end, h0)).astype(BF16)
            cd_scr[:, lanes] = jnp.where(low1, e_last[:, h0:h0 + 1], e_last[:, h0 + 1:h0 + 2])
        new = jnp.dot(bmt_g, xdte_scr[:, g * gw:(g + 1) * gw], preferred_element_type=F32)
        state_scr[g] = state_scr[g] * cd_scr[:, g * gw:(g + 1) * gw] + new

    yz = y_scr[...] * _silu(z_ref[0])
    for g in range(g_n):
        blk = yz[:, g * gw:(g + 1) * gw]
        ms = jnp.mean(blk * blk, axis=-1, keepdims=True)
        o_ref[0, :, g * gw:(g + 1) * gw] = (blk * lax.rsqrt(ms + RMS_EPS)
                                            * ng_ref[:, g * gw:(g + 1) * gw]).astype(BF16)


def _ssd_scan(xbc, z, dt, dtt, cw, cb, dtb, dtbt, alog, alogt, dsk, ng, heads):
    bsz, seq, conv_dim = xbc.shape
    inner = z.shape[2]
    ln = SSD_CHUNK
    lanes = dt.shape[2]
    gw = inner // SSM_GROUPS
    blk = lambda n: pl.BlockSpec((1, ln, n), lambda bi, c: (bi, c, 0))
    est = (2 * ln * (conv_dim + inner) * 4 + 2 * ln * inner * 2 + (ln + 8) * conv_dim * 4
           + SSM_GROUPS * SSM_STATE * gw * 4 + ln * inner * 6 + 12 * ln * conv_dim * 4 + (8 << 20))
    return pl.pallas_call(
        functools.partial(_ssd_scan_kernel, inner=inner, heads=heads),
        grid=(bsz, seq // ln),
        in_specs=[blk(conv_dim), blk(inner), blk(lanes),
                  pl.BlockSpec((1, lanes, ln), lambda bi, c: (bi, 0, c)),
                  _resident(cw.shape), _resident(cb.shape), _resident(dtb.shape), _resident(dtbt.shape),
                  _resident(alog.shape), _resident(alogt.shape), _resident(dsk.shape), _resident(ng.shape)],
        out_specs=blk(inner),
        out_shape=jax.ShapeDtypeStruct((bsz, seq, inner), BF16),
        scratch_shapes=[pltpu.VMEM((ln + V7X_SUBLANES, conv_dim), F32),
                        pltpu.VMEM((SSM_GROUPS, SSM_STATE, gw), F32),
                        pltpu.VMEM((ln, inner), BF16),
                        pltpu.VMEM((ln, inner), F32),
                        pltpu.VMEM((1, inner), F32)],
        compiler_params=pltpu.CompilerParams(dimension_semantics=("arbitrary", "arbitrary"),
                                             vmem_limit_bytes=_vmem_limit(est)),
        name="ssd_scan",
    )(xbc, z, dt, dtt, cw, cb, dtb, dtbt, alog, alogt, dsk, ng)


def _ffn_weights(w_in, w_out):
    n_l, d, f2 = w_in.shape
    f = f2 // 2
    ck = V7X_MXU_DIM if f % V7X_MXU_DIM == 0 else V7X_LANES
    n = f // ck
    win = w_in.astype(BF16).reshape(n_l, d, 2, n, ck).transpose(0, 2, 3, 1, 4).reshape(n_l, 2 * n, d, ck)
    wout = w_out.astype(BF16).reshape(n_l, n, ck, d)
    return win, wout


def _dsa_layer(xt, bsz, seq, w_in, kvg, w_uk, w_uv, w_out, g, b, alpha):
    d = xt.shape[1]
    o1 = ATTN_HEADS * ATTN_HEAD_DIM
    o2 = o1 + KV_RANK
    o3 = o2 + IDX_HEADS * IDX_HEAD_DIM
    wb = w_in.astype(BF16)
    pad = V7X_LANES - IDX_HEAD_DIM - IDX_HEADS
    wkw = jnp.pad(wb[:, o3:], ((0, 0), (0, pad)))
    q, ckv, qi, ki, wi = _attn_inproj(xt, wb[:, :o1], wb[:, o1:o2], wb[:, o2:o3], wkw, kvg.reshape(1, -1))
    per_seq = lambda v: v.reshape(bsz, seq, -1)
    per_seq_t = lambda v: jnp.swapaxes(per_seq(v), 1, 2)
    out = _dsa_attn(per_seq(xt), per_seq_t(q), per_seq_t(qi), per_seq_t(wi), per_seq(ckv), per_seq_t(ckv),
                    per_seq(ki), jnp.swapaxes(w_uk, 1, 2).astype(BF16), jnp.swapaxes(w_uv, 1, 2).astype(BF16),
                    w_out.astype(BF16), g, b, alpha)
    return out.reshape(bsz * seq, d)


def _ssd_layer(xt, bsz, seq, w_in, conv_w, conv_b, dt_bias, a_log, d_skip, norm_g, w_out, g, b, alpha):
    inner = norm_g.shape[0]
    heads = dt_bias.shape[0]
    conv_dim = conv_w.shape[1]
    lanes = V7X_LANES * pl.cdiv(heads, V7X_LANES)
    wb = w_in.astype(BF16)
    wdt = jnp.pad(wb[:, inner + conv_dim:], ((0, 0), (0, lanes - heads)))
    z, xbc, dt = _ssd_inproj(xt, wb[:, :inner], wb[:, inner:inner + conv_dim], wdt)
    dt = dt.reshape(bsz, seq, lanes)
    padh = lambda v: jnp.pad(v, (0, lanes - heads))
    dtb, alog = padh(dt_bias), padh(a_log)
    yz = _ssd_scan(xbc.reshape(bsz, seq, conv_dim), z.reshape(bsz, seq, inner), dt, jnp.swapaxes(dt, 1, 2),
                   conv_w, conv_b.reshape(1, -1), dtb.reshape(1, -1), dtb.reshape(-1, 1),
                   alog.reshape(1, -1), alog.reshape(-1, 1),
                   jnp.repeat(d_skip, SSM_HEAD_DIM).reshape(1, -1), norm_g.reshape(1, -1), heads)
    return _proj_ln(xt, yz.reshape(bsz * seq, inner), w_out.astype(BF16), g, b, alpha)


def kernel(x, ln_g, ln_b, ffn1_w_in, ffn1_w_out, ffn2_w_in, ffn2_w_out, attn_w_in, attn_kv_norm, attn_w_uk,
           attn_w_uv, attn_w_out, ssm_w_in, ssm_conv_w, ssm_conv_b, ssm_dt_bias, ssm_a_log, ssm_d,
           ssm_norm_g, ssm_w_out):
    bsz, seq, d = x.shape
    depth = ln_g.shape[0]
    alpha = (2.0 * depth) ** 0.25
    f1_in, f1_out = _ffn_weights(ffn1_w_in, ffn1_w_out)
    f2_in, f2_out = _ffn_weights(ffn2_w_in, ffn2_w_out)
    ln = lambda i, k: (ln_g[i, k].reshape(1, d), ln_b[i, k].reshape(1, d))
    xt = x.reshape(bsz * seq, d)
    for i in range(depth):
        xt = _ffn_ln(xt, f1_in[i], f1_out[i], *ln(i, 0), alpha)
        j = i // 2
        if i % 2 == 0:
            xt = _dsa_layer(xt, bsz, seq, attn_w_in[j], attn_kv_norm[j], attn_w_uk[j], attn_w_uv[j],
                            attn_w_out[j], *ln(i, 1), alpha)
        else:
            xt = _ssd_layer(xt, bsz, seq, ssm_w_in[j], ssm_conv_w[j], ssm_conv_b[j], ssm_dt_bias[j],
                            ssm_a_log[j], ssm_d[j], ssm_norm_g[j], ssm_w_out[j], *ln(i, 1), alpha)
        xt = _ffn_ln(xt, f2_in[i], f2_out[i], *ln(i, 2), alpha)
    return xt.reshape(bsz, seq, d)
```

```python
import functools
import math

import jax
import jax.numpy as jnp
from jax import lax
from jax.experimental import pallas as pl
from jax.experimental.pallas import tpu as pltpu

F32 = jnp.float32
BF16 = jnp.bfloat16
I32 = jnp.int32

ATTN_HEADS = 16
ATTN_HEAD_DIM = 64
KV_RANK = 256
IDX_HEADS = 8
IDX_HEAD_DIM = 64
TOPK_MAX = 256
SSM_HEAD_DIM = 64
SSM_GROUPS = 4
SSM_STATE = 128
SSM_CONV = 4
LN_EPS = 1e-5
RMS_EPS = 1e-6

V7X_LANES = 128
V7X_SUBLANES = 8
V7X_MXU_DIM = 256
V7X_VMEM_BYTES = 64 * 1024 * 1024

ROW_TILE = 512
Q_TILE = 128
IDX_KEY_TILE = 512
SCORE_WIDTHS = (4, 2, 1)
ATTN_KEY_TILE = 2 * V7X_MXU_DIM
ATTN_WIDTHS = (8, 4, 2, 1)
COUNT_ROWS = 64
SEARCH_FIXED_BITS = 19
SEARCH_GROUP_BITS = 4
SSD_CHUNK = 128
MASKED_LOGIT = -1e30
SAFE_LOGIT_BOUND = 40.0
NORM_SLACK = 1.01
LOG2_E = math.log2(math.e)
INT_MIN = -(2 ** 31)
NEG_INF_KEY = INT_MIN + 0x007FFFFF


def _vmem_limit(nbytes):
    return int(min(nbytes, V7X_VMEM_BYTES - 8 * 1024 * 1024))


def _resident(shape):
    nd = len(shape)
    return pl.BlockSpec(shape, lambda *_: (0,) * nd, pipeline_mode=pl.Buffered(1))


def _layer_norm(y, g, b):
    mu = jnp.mean(y, axis=-1, keepdims=True)
    d = y - mu
    var = jnp.mean(d * d, axis=-1, keepdims=True)
    return d * lax.rsqrt(var + LN_EPS) * g + b


def _silu(v):
    return v * jax.nn.sigmoid(v)


def _ffn_ln_kernel(x_ref, win_ref, wout_ref, g_ref, b_ref, o_ref, acc_ref, *, n_chunk, alpha):
    x = x_ref[...]
    xb = x.astype(BF16)
    for c in range(n_chunk):
        gate = jnp.dot(xb, win_ref[c], preferred_element_type=F32)
        up = jnp.dot(xb, win_ref[n_chunk + c], preferred_element_type=F32)
        act = (_silu(gate) * up).astype(BF16)
        part = jnp.dot(act, wout_ref[c], preferred_element_type=F32)
        if c == 0:
            acc_ref[...] = part
        else:
            acc_ref[...] += part
    y = alpha * x + 0.5 * acc_ref[...]
    o_ref[...] = _layer_norm(y, g_ref[...], b_ref[...])


def _ffn_ln(x, win, wout, g, b, alpha):
    t, d = x.shape
    n2, _, ck = win.shape
    n_chunk = n2 // 2
    tm = min(ROW_TILE, t)
    est = (win.size + wout.size) * 2 + 5 * tm * d * 4 + 6 * tm * ck * 4 + (8 << 20)
    return pl.pallas_call(
        functools.partial(_ffn_ln_kernel, n_chunk=n_chunk, alpha=alpha),
        grid=(t // tm,),
        in_specs=[pl.BlockSpec((tm, d), lambda i: (i, 0)),
                  _resident(win.shape), _resident(wout.shape),
                  _resident(g.shape), _resident(b.shape)],
        out_specs=pl.BlockSpec((tm, d), lambda i: (i, 0)),
        out_shape=jax.ShapeDtypeStruct((t, d), F32),
        scratch_shapes=[pltpu.VMEM((tm, d), F32)],
        compiler_params=pltpu.CompilerParams(dimension_semantics=("arbitrary",),
                                             vmem_limit_bytes=_vmem_limit(est)),
        name="ffn_ln",
    )(x, win, wout, g, b)


def _proj_ln_kernel(x_ref, a_ref, w_ref, g_ref, b_ref, o_ref, *, alpha):
    m = jnp.dot(a_ref[...], w_ref[...], preferred_element_type=F32)
    o_ref[...] = _layer_norm(alpha * x_ref[...] + m, g_ref[...], b_ref[...])


def _proj_ln(x, a, w, g, b, alpha):
    t, d = x.shape
    k = a.shape[1]
    tm = min(ROW_TILE, t)
    est = w.size * 2 + 6 * tm * d * 4 + 2 * tm * k * 2 + (8 << 20)
    return pl.pallas_call(
        functools.partial(_proj_ln_kernel, alpha=alpha),
        grid=(t // tm,),
        in_specs=[pl.BlockSpec((tm, d), lambda i: (i, 0)),
                  pl.BlockSpec((tm, k), lambda i: (i, 0)),
                  _resident(w.shape), _resident(g.shape), _resident(b.shape)],
        out_specs=pl.BlockSpec((tm, d), lambda i: (i, 0)),
        out_shape=jax.ShapeDtypeStruct((t, d), F32),
        compiler_params=pltpu.CompilerParams(dimension_semantics=("arbitrary",),
                                             vmem_limit_bytes=_vmem_limit(est)),
        name="proj_ln",
    )(x, a, w, g, b)


def _attn_inproj_kernel(x_ref, wq_ref, wc_ref, wqi_ref, wkw_ref, kvg_ref,
                        q_ref, c_ref, qi_ref, ki_ref, wi_ref, *, w_scale):
    xb = x_ref[...].astype(BF16)
    q_ref[...] = jnp.dot(xb, wq_ref[...], preferred_element_type=F32).astype(BF16)
    c = jnp.dot(xb, wc_ref[...], preferred_element_type=F32)
    ms = jnp.mean(c * c, axis=-1, keepdims=True)
    c_ref[...] = (c * lax.rsqrt(ms + RMS_EPS) * kvg_ref[...]).astype(BF16)
    qi_ref[...] = jnp.dot(xb, wqi_ref[...], preferred_element_type=F32).astype(BF16)
    kw = jnp.dot(xb, wkw_ref[...], preferred_element_type=F32)
    ki_ref[...] = kw[:, :IDX_HEAD_DIM].astype(BF16)
    wi_ref[...] = kw[:, IDX_HEAD_DIM:IDX_HEAD_DIM + IDX_HEADS] * (IDX_HEADS ** -0.5) * w_scale


def _attn_inproj(x, wq, wc, wqi, wkw, kvg):
    t, d = x.shape
    tm = min(ROW_TILE, t)
    nq, nc, nqi = wq.shape[1], wc.shape[1], wqi.shape[1]
    est = (wq.size + wc.size + wqi.size + wkw.size) * 2 + 2 * tm * d * 4 + 8 * tm * (nq + nc + nqi) + (8 << 20)
    row = lambda n: pl.BlockSpec((tm, n), lambda i: (i, 0))
    return pl.pallas_call(
        functools.partial(_attn_inproj_kernel, w_scale=IDX_HEAD_DIM ** -0.5),
        grid=(t // tm,),
        in_specs=[row(d), _resident(wq.shape), _resident(wc.shape), _resident(wqi.shape),
                  _resident(wkw.shape), _resident(kvg.shape)],
        out_specs=[row(nq), row(nc), row(nqi), row(IDX_HEAD_DIM), row(IDX_HEADS)],
        out_shape=[jax.ShapeDtypeStruct((t, nq), BF16), jax.ShapeDtypeStruct((t, nc), BF16),
                   jax.ShapeDtypeStruct((t, nqi), BF16), jax.ShapeDtypeStruct((t, IDX_HEAD_DIM), BF16),
                   jax.ShapeDtypeStruct((t, IDX_HEADS), F32)],
        compiler_params=pltpu.CompilerParams(dimension_semantics=("arbitrary",),
                                             vmem_limit_bytes=_vmem_limit(est)),
        name="attn_inproj",
    )(x, wq, wc, wqi, wkw, kvg)


def _float_key(v):
    bits = pltpu.bitcast(v, I32)
    return bits ^ ((bits >> 31) & 0x7FFFFFFF)


def _store_keys(key_scr, off, score, causal):
    key_scr[pl.ds(off, score.shape[0]), :] = _float_key(jnp.where(causal, score, -jnp.inf))


def _topk_threshold(key_scr, thr_scr, lim_scr, cnt_scr, n_blk, n_causal, *, topk, seq):
    tki = IDX_KEY_TILE
    tq = key_scr.shape[1]
    kf = float(topk)

    def count(pred):
        def body(kb, c):
            off = pl.multiple_of(kb * tki, tki)
            keys = key_scr[pl.ds(off, tki), :]
            k_pos = off + lax.broadcasted_iota(I32, (tki, tq), 0)
            hit = jnp.where(pred(keys, k_pos), 1.0, 0.0)
            return c + jnp.sum(hit.reshape(tki // COUNT_ROWS, COUNT_ROWS, tq), axis=0)
        c = lax.fori_loop(0, n_blk, body, jnp.zeros((COUNT_ROWS, tq), F32))
        return jnp.sum(c, axis=0, keepdims=True)

    def add_bit(j, st):
        thr, c_thr = st
        cand = thr + jnp.left_shift(jnp.int32(1), 30 - j)
        c = count(lambda k, p: k >= cand)
        ok = c >= kf
        return jnp.where(ok, cand, thr), jnp.where(ok, c, c_thr)

    take_all = n_causal <= kf
    c0 = count(lambda k, p: k >= 0)
    n_all = jnp.full((1, tq), n_blk * tki).astype(F32)
    thr, c_thr = lax.fori_loop(
        0, SEARCH_FIXED_BITS, add_bit,
        (jnp.where(c0 >= kf, 0, INT_MIN).astype(I32), jnp.where(c0 >= kf, c0, n_all)))
    thr_scr[...] = thr
    cnt_scr[...] = c_thr
    for lo in range(SEARCH_FIXED_BITS, 31, SEARCH_GROUP_BITS):
        @pl.when(jnp.max(jnp.where((cnt_scr[...] != kf) & ~take_all, 1.0, 0.0)) > 0.5)
        def _(lo=lo):
            thr, c_thr = lax.fori_loop(lo, min(lo + SEARCH_GROUP_BITS, 31), add_bit, (thr_scr[...], cnt_scr[...]))
            thr_scr[...] = thr
            cnt_scr[...] = c_thr
    c_thr = cnt_scr[...]
    thr = jnp.maximum(thr_scr[...], NEG_INF_KEY + 1)
    tied = (c_thr > kf) & ~take_all
    thr_scr[...] = thr
    lim_scr[...] = jnp.full((1, tq), seq, I32)

    @pl.when(jnp.max(jnp.where(tied, 1.0, 0.0)) > 0.5)
    def _():
        n_bits = max(1, (seq - 1).bit_length())
        need = kf - count(lambda k, p: k > thr)

        def lim_body(j, lim):
            cand = lim + jnp.left_shift(jnp.int32(1), n_bits - 1 - j)
            return jnp.where(count(lambda k, p: (k == thr) & (p < cand)) < need, cand, lim)

        lim = lax.fori_loop(0, n_bits, lim_body, jnp.zeros((1, tq), I32))
        lim_scr[...] = jnp.where(tied, lim, seq)


def _dsa_kernel(x_ref, q_ref, qi_ref, wit_ref, ckv_ref, ckvt_ref, ki_ref, wukt_ref, wuvt_ref, wo_ref,
                g_ref, b_ref, cmax_ref, o_ref,
                key_scr, qabs_scr, qipair_scr, acc_scr, m_scr, l_scr, bound_scr, ot_scr, thr_scr, lim_scr, cnt_scr,
                *, topk, seq, alpha):
    i = pl.program_id(1)
    tq, h_n, dh, r_n = Q_TILE, ATTN_HEADS, ATTN_HEAD_DIM, KV_RANK
    hi_n, di = IDX_HEADS, IDX_HEAD_DIM
    tki, tk = IDX_KEY_TILE, ATTN_KEY_TILE
    pair = 2 * tq
    q0 = i * tq
    n_kv_blk = q0 // tk + 1
    n_idx_blk = pl.cdiv(n_kv_blk * tk, tki)

    for h in range(h_n):
        qa = lax.dot_general(wukt_ref[h], q_ref[0, :, h * dh:(h + 1) * dh], (((1,), (1,)), ((), ())),
                             preferred_element_type=F32)
        qb = (qa * (dh ** -0.5 * LOG2_E)).astype(BF16)
        qabs_scr[:, h * tq:(h + 1) * tq] = qb
        qf = qb.astype(F32)
        bound_scr[:, h * tq:(h + 1) * tq] = jnp.sqrt(jnp.sum(qf * qf, axis=0, keepdims=True)) * cmax_ref[...]
    for h in range(hi_n):
        qipair_scr[h // 2, (h % 2) * tq:(h % 2 + 1) * tq, :] = qi_ref[0, :, h * di:(h + 1) * di]

    wit = wit_ref[0]
    q_pos = q0 + lax.broadcasted_iota(I32, (1, tq), 1)

    def score_rows(off, rows):
        ki = ki_ref[0, pl.ds(off, rows), :]
        score = jnp.zeros((rows, tq), F32)
        for j in range(hi_n // 2):
            s = lax.dot_general(ki, qipair_scr[j], (((1,), (1,)), ((), ())),
                                preferred_element_type=F32)
            for e in range(2):
                h = 2 * j + e
                score = score + jnp.maximum(s[:, e * tq:(e + 1) * tq], 0.0) * wit[h:h + 1, :]
        k_pos = off + lax.broadcasted_iota(I32, (rows, tq), 0)
        _store_keys(key_scr, off, score, k_pos <= q_pos)

    done = 0
    for width in SCORE_WIDTHS:
        n = (n_idx_blk - done) // width

        def score_body(kb, carry, done=done, width=width):
            score_rows(pl.multiple_of((done + kb * width) * tki, tki), width * tki)
            return carry

        lax.fori_loop(0, n, score_body, 0)
        done = done + n * width
    _topk_threshold(key_scr, thr_scr, lim_scr, cnt_scr, n_idx_blk, (q_pos + 1).astype(F32), topk=topk, seq=seq)
    thr = thr_scr[...]
    lim = lim_scr[...]

    m_scr[...] = jnp.full(m_scr.shape, MASKED_LOGIT, F32)
    l_scr[...] = jnp.zeros(l_scr.shape, F32)
    acc_scr[...] = jnp.zeros(acc_scr.shape, F32)

    def attn_block(off, tk, running_max):
        keys = key_scr[pl.ds(off, tk), :]
        k_pos = off + lax.broadcasted_iota(I32, (tk, tq), 0)
        sel = (keys >= thr) & ((keys > thr) | (k_pos <= lim))
        bias = jnp.where(sel, 0.0, MASKED_LOGIT)
        bias = jnp.concatenate([bias, bias], axis=1)
        c_blk = ckv_ref[0, pl.ds(off, tk), :]
        ct_blk = ckvt_ref[0, :, pl.ds(off, tk)]
        for j in range(h_n // 2):
            cols = slice(j * pair, (j + 1) * pair)
            lg = jnp.dot(c_blk, qabs_scr[:, cols], preferred_element_type=F32) + bias
            if running_max:
                m_old = m_scr[:, cols]
                m_new = jnp.maximum(m_old, jnp.max(lg, axis=0, keepdims=True))
                a = jnp.exp2(m_old - m_new)
                p = jnp.exp2(lg - m_new)
                l_scr[:, cols] = a * l_scr[:, cols] + jnp.sum(p, axis=0, keepdims=True)
                m_scr[:, cols] = m_new
                pv = jnp.dot(ct_blk, p.astype(BF16), preferred_element_type=F32)
                acc_scr[:, cols] = acc_scr[:, cols] * a + pv
            else:
                p = jnp.exp2(lg - bound_scr[:, cols])
                l_scr[:, cols] += jnp.sum(p, axis=0, keepdims=True)
                acc_scr[:, cols] += jnp.dot(ct_blk, p.astype(BF16), preferred_element_type=F32)

    def attn_loop(done, width, running_max):
        n = (n_kv_blk - done) // width

        def body(kb, carry):
            attn_block(pl.multiple_of((done + kb * width) * tk, tk), width * tk, running_max)
            return carry

        lax.fori_loop(0, n, body, 0)
        return done + n * width

    bounded = jnp.max(bound_scr[...]) < SAFE_LOGIT_BOUND

    @pl.when(bounded)
    def _():
        done = 0
        for width in ATTN_WIDTHS:
            done = attn_loop(done, width, False)

    @pl.when(jnp.logical_not(bounded))
    def _():
        attn_loop(0, 1, True)

    for h in range(h_n):
        cols = slice(h * tq, (h + 1) * tq)
        o_lat = (acc_scr[:, cols] / l_scr[:, cols]).astype(BF16)
        ot_scr[h * dh:(h + 1) * dh, :] = jnp.dot(wuvt_ref[h], o_lat, preferred_element_type=F32)
    o = ot_scr[...].T.astype(BF16)
    mix = jnp.dot(o, wo_ref[...], preferred_element_type=F32)
    o_ref[0] = _layer_norm(alpha * x_ref[0] + mix, g_ref[...], b_ref[...])


def _dsa_attn(x, q, qi, wit, ckv, ckvt, ki, wukt, wuvt, wo, g, b, cmax, alpha):
    bsz, seq, d = x.shape
    tq = Q_TILE
    topk = min(TOPK_MAX, seq // 4)
    h_n, r_n = ATTN_HEADS, KV_RANK
    qblk = lambda n: pl.BlockSpec((1, tq, n), lambda bi, i: (bi, i, 0))
    qblk_t = lambda n: pl.BlockSpec((1, n, tq), lambda bi, i: (bi, 0, i))
    per_batch = lambda shp: pl.BlockSpec((1,) + shp, lambda bi, i: (bi, 0, 0), pipeline_mode=pl.Buffered(1))
    attn_rows = ATTN_WIDTHS[0] * ATTN_KEY_TILE
    score_rows = SCORE_WIDTHS[0] * IDX_KEY_TILE
    est = (2 * seq * r_n * 2 + seq * V7X_LANES * 2 + tq * seq * 4
           + (wukt.size + wuvt.size + wo.size) * 2
           + h_n * tq * r_n * 6 + (attn_rows + score_rows) * 2 * tq * 12
           + 2 * h_n * ATTN_HEAD_DIM * tq * (2 + 4) + 6 * tq * d * 4 + (8 << 20))
    return pl.pallas_call(
        functools.partial(_dsa_kernel, topk=topk, seq=seq, alpha=alpha),
        grid=(bsz, seq // tq),
        in_specs=[qblk(d), qblk(q.shape[2]), qblk(qi.shape[2]), qblk_t(wit.shape[1]),
                  per_batch((seq, r_n)), per_batch((r_n, seq)), per_batch((seq, IDX_HEAD_DIM)),
                  _resident(wukt.shape), _resident(wuvt.shape), _resident(wo.shape),
                  _resident(g.shape), _resident(b.shape), _resident(cmax.shape)],
        out_specs=qblk(d),
        out_shape=jax.ShapeDtypeStruct((bsz, seq, d), F32),
        scratch_shapes=[pltpu.VMEM((seq, tq), I32),
                        pltpu.VMEM((r_n, h_n * tq), BF16),
                        pltpu.VMEM((IDX_HEADS // 2, 2 * tq, IDX_HEAD_DIM), BF16),
                        pltpu.VMEM((r_n, h_n * tq), F32),
                        pltpu.VMEM((1, h_n * tq), F32),
                        pltpu.VMEM((1, h_n * tq), F32),
                        pltpu.VMEM((1, h_n * tq), F32),
                        pltpu.VMEM((h_n * ATTN_HEAD_DIM, tq), F32),
                        pltpu.VMEM((1, tq), I32),
                        pltpu.VMEM((1, tq), I32),
                        pltpu.VMEM((1, tq), F32)],
        compiler_params=pltpu.CompilerParams(dimension_semantics=("arbitrary", "arbitrary"),
                                             vmem_limit_bytes=_vmem_limit(est)),
        name="dsa_attn",
    )(x, q, qi, wit, ckv, ckvt, ki, wukt, wuvt, wo, g, b, cmax)


def _ssd_inproj_kernel(x_ref, wz_ref, wx_ref, wdt_ref, z_ref, xbc_ref, dt_ref):
    xb = x_ref[...].astype(BF16)
    z_ref[...] = jnp.dot(xb, wz_ref[...], preferred_element_type=F32)
    xbc_ref[...] = jnp.dot(xb, wx_ref[...], preferred_element_type=F32)
    dt_ref[...] = jnp.dot(xb, wdt_ref[...], preferred_element_type=F32)


def _ssd_inproj(x, wz, wx, wdt):
    t, d = x.shape
    tm = min(ROW_TILE, t)
    nz, nx, nd = wz.shape[1], wx.shape[1], wdt.shape[1]
    est = (wz.size + wx.size + wdt.size) * 2 + 2 * tm * d * 4 + 3 * tm * (nz + nx + nd) * 4 + (8 << 20)
    row = lambda n: pl.BlockSpec((tm, n), lambda i: (i, 0))
    return pl.pallas_call(
        _ssd_inproj_kernel,
        grid=(t // tm,),
        in_specs=[row(d), _resident(wz.shape), _resident(wx.shape), _resident(wdt.shape)],
        out_specs=[row(nz), row(nx), row(nd)],
        out_shape=[jax.ShapeDtypeStruct((t, nz), F32), jax.ShapeDtypeStruct((t, nx), F32),
                   jax.ShapeDtypeStruct((t, nd), F32)],
        compiler_params=pltpu.CompilerParams(dimension_semantics=("arbitrary",),
                                             vmem_limit_bytes=_vmem_limit(est)),
        name="ssd_inproj",
    )(x, wz, wx, wdt)


def _cumsum(a, axis):
    n = a.shape[axis]
    idx = lax.broadcasted_iota(I32, a.shape, axis)
    s = 1
    while s < n:
        a = a + jnp.where(idx >= s, pltpu.roll(a, s, axis), 0.0)
        s *= 2
    return a


def _softplus(v):
    return jnp.maximum(v, 0.0) + jnp.log1p(jnp.exp(-jnp.abs(v)))


def _ssd_scan_kernel(xbc_ref, z_ref, dt_ref, dtt_ref, cw_ref, cb_ref, dtb_ref, dtbt_ref, alog_ref, alogt_ref,
                     dsk_ref, ng_ref, o_ref,
                     win_scr, state_scr, xdte_scr, y_scr, cd_scr, *, inner):
    c_idx = pl.program_id(1)
    ln = SSD_CHUNK
    g_n, n_st, p_dim = SSM_GROUPS, SSM_STATE, SSM_HEAD_DIM
    gw = inner // g_n
    pairs_per_group = gw // V7X_LANES
    tail = V7X_SUBLANES

    @pl.when(c_idx == 0)
    def _():
        win_scr[0:tail, :] = jnp.zeros((tail, win_scr.shape[1]), F32)
        state_scr[...] = jnp.zeros(state_scr.shape, F32)

    xbc = xbc_ref[0]
    win_scr[tail:tail + ln, :] = xbc
    conv = cb_ref[...] + cw_ref[0:1, :] * win_scr[tail - SSM_CONV + 1:tail - SSM_CONV + 1 + ln, :]
    for k in range(1, SSM_CONV):
        lo = tail - SSM_CONV + 1 + k
        conv = conv + cw_ref[k:k + 1, :] * win_scr[lo:lo + ln, :]
    win_scr[0:tail, :] = xbc[ln - tail:ln, :]
    act = _silu(conv)
    xs = act[:, :inner]
    bm = act[:, inner:inner + g_n * n_st]
    cm = act[:, inner + g_n * n_st:]

    a_row = -jnp.exp(alog_ref[...])
    a_col = -jnp.exp(alogt_ref[...])
    dt = _softplus(dt_ref[0] + dtb_ref[...])
    dtt = _softplus(dtt_ref[0] + dtbt_ref[...])
    acs = _cumsum(dt * a_row, 0)
    acst = _cumsum(dtt * a_col, 1)
    e_acs = jnp.exp(acs)
    last = acs[ln - 1:ln, :]
    e_end = jnp.exp(last - acs)
    e_last = jnp.exp(last)

    tri = lax.broadcasted_iota(I32, (ln, ln), 0) >= lax.broadcasted_iota(I32, (ln, ln), 1)
    low = lax.broadcasted_iota(I32, (ln, V7X_LANES), 1) < p_dim
    low1 = low[0:1, :]

    def pair_cols(v, h0):
        return jnp.where(low, v[:, h0:h0 + 1], v[:, h0 + 1:h0 + 2])

    for g in range(g_n):
        bm_g = bm[:, g * n_st:(g + 1) * n_st]
        cm_g = cm[:, g * n_st:(g + 1) * n_st].astype(BF16)
        bmt_g = bm_g.T.astype(BF16)
        cb = jnp.dot(cm_g, bmt_g, preferred_element_type=F32)
        y_off = jnp.dot(cm_g, state_scr[g].astype(BF16), preferred_element_type=F32)
        for jp in range(pairs_per_group):
            j = g * pairs_per_group + jp
            h0 = 2 * j
            lanes = slice(j * V7X_LANES, (j + 1) * V7X_LANES)
            d0 = jnp.where(tri, jnp.exp(acs[:, h0:h0 + 1] - acst[h0:h0 + 1, :]), 0.0)
            d1 = jnp.where(tri, jnp.exp(acs[:, h0 + 1:h0 + 2] - acst[h0 + 1:h0 + 2, :]), 0.0)
            m_pair = jnp.concatenate([(cb * d0).astype(BF16), (cb * d1).astype(BF16)], axis=1)
            x_pair = xs[:, lanes]
            xdt = x_pair * pair_cols(dt, h0)
            xb = xdt.astype(BF16)
            zero = jnp.zeros_like(xb)
            rhs = jnp.concatenate([jnp.where(low, xb, zero), jnp.where(low, zero, xb)], axis=0)
            y_diag = jnp.dot(m_pair, rhs, preferred_element_type=F32)
            y = y_diag + y_off[:, jp * V7X_LANES:(jp + 1) * V7X_LANES] * pair_cols(e_acs, h0)
            y_scr[:, lanes] = y + x_pair * dsk_ref[:, lanes]
            xdte_scr[:, lanes] = (xdt * pair_cols(e_end, h0)).astype(BF16)
            cd_scr[:, lanes] = jnp.where(low1, e_last[:, h0:h0 + 1], e_last[:, h0 + 1:h0 + 2])
        new = jnp.dot(bmt_g, xdte_scr[:, g * gw:(g + 1) * gw], preferred_element_type=F32)
        state_scr[g] = state_scr[g] * cd_scr[:, g * gw:(g + 1) * gw] + new

    yz = y_scr[...] * _silu(z_ref[0])
    for g in range(g_n):
        blk = yz[:, g * gw:(g + 1) * gw]
        ms = jnp.mean(blk * blk, axis=-1, keepdims=True)
        o_ref[0, :, g * gw:(g + 1) * gw] = (blk * lax.rsqrt(ms + RMS_EPS)
                                            * ng_ref[:, g * gw:(g + 1) * gw]).astype(BF16)


def _ssd_scan(xbc, z, dt, dtt, cw, cb, dtb, dtbt, alog, alogt, dsk, ng):
    bsz, seq, conv_dim = xbc.shape
    inner = z.shape[2]
    ln = SSD_CHUNK
    lanes = dt.shape[2]
    gw = inner // SSM_GROUPS
    blk = lambda n: pl.BlockSpec((1, ln, n), lambda bi, c: (bi, c, 0))
    est = (2 * ln * (conv_dim + inner) * 4 + 2 * ln * inner * 2 + (ln + 8) * conv_dim * 4
           + SSM_GROUPS * SSM_STATE * gw * 4 + ln * inner * 6 + 12 * ln * conv_dim * 4 + (8 << 20))
    return pl.pallas_call(
        functools.partial(_ssd_scan_kernel, inner=inner),
        grid=(bsz, seq // ln),
        in_specs=[blk(conv_dim), blk(inner), blk(lanes),
                  pl.BlockSpec((1, lanes, ln), lambda bi, c: (bi, 0, c)),
                  _resident(cw.shape), _resident(cb.shape), _resident(dtb.shape), _resident(dtbt.shape),
                  _resident(alog.shape), _resident(alogt.shape), _resident(dsk.shape), _resident(ng.shape)],
        out_specs=blk(inner),
        out_shape=jax.ShapeDtypeStruct((bsz, seq, inner), BF16),
        scratch_shapes=[pltpu.VMEM((ln + V7X_SUBLANES, conv_dim), F32),
                        pltpu.VMEM((SSM_GROUPS, SSM_STATE, gw), F32),
                        pltpu.VMEM((ln, inner), BF16),
                        pltpu.VMEM((ln, inner), F32),
                        pltpu.VMEM((1, inner), F32)],
        compiler_params=pltpu.CompilerParams(dimension_semantics=("arbitrary", "arbitrary"),
                                             vmem_limit_bytes=_vmem_limit(est)),
        name="ssd_scan",
    )(xbc, z, dt, dtt, cw, cb, dtb, dtbt, alog, alogt, dsk, ng)


def _ffn_weights(w_in, w_out):
    n_l, d, f2 = w_in.shape
    f = f2 // 2
    ck = V7X_MXU_DIM if f % V7X_MXU_DIM == 0 else V7X_LANES
    n = f // ck
    win = w_in.astype(BF16).reshape(n_l, d, 2, n, ck).transpose(0, 2, 3, 1, 4).reshape(n_l, 2 * n, d, ck)
    wout = w_out.astype(BF16).reshape(n_l, n, ck, d)
    return win, wout


def _dsa_layer(xt, bsz, seq, w_in, kvg, w_uk, w_uv, w_out, g, b, alpha):
    d = xt.shape[1]
    o1 = ATTN_HEADS * ATTN_HEAD_DIM
    o2 = o1 + KV_RANK
    o3 = o2 + IDX_HEADS * IDX_HEAD_DIM
    wb = w_in.astype(BF16)
    pad = V7X_LANES - IDX_HEAD_DIM - IDX_HEADS
    wkw = jnp.pad(wb[:, o3:], ((0, 0), (0, pad)))
    q, ckv, qi, ki, wi = _attn_inproj(xt, wb[:, :o1], wb[:, o1:o2], wb[:, o2:o3], wkw, kvg.reshape(1, -1))
    cmax = (jnp.max(jnp.abs(kvg)) * (KV_RANK ** 0.5 * NORM_SLACK)).reshape(1, 1)
    per_seq = lambda v: v.reshape(bsz, seq, -1)
    per_seq_t = lambda v: jnp.swapaxes(per_seq(v), 1, 2)
    out = _dsa_attn(per_seq(xt), per_seq(q), per_seq(qi), per_seq_t(wi), per_seq(ckv), per_seq_t(ckv),
                    per_seq(ki), jnp.swapaxes(w_uk, 1, 2).astype(BF16), jnp.swapaxes(w_uv, 1, 2).astype(BF16),
                    w_out.astype(BF16), g, b, cmax, alpha)
    return out.reshape(bsz * seq, d)


def _ssd_layer(xt, bsz, seq, w_in, conv_w, conv_b, dt_bias, a_log, d_skip, norm_g, w_out, g, b, alpha):
    inner = norm_g.shape[0]
    heads = dt_bias.shape[0]
    conv_dim = conv_w.shape[1]
    lanes = V7X_LANES * pl.cdiv(heads, V7X_LANES)
    wb = w_in.astype(BF16)
    wdt = jnp.pad(wb[:, inner + conv_dim:], ((0, 0), (0, lanes - heads)))
    z, xbc, dt = _ssd_inproj(xt, wb[:, :inner], wb[:, inner:inner + conv_dim], wdt)
    dt = dt.reshape(bsz, seq, lanes)
    padh = lambda v: jnp.pad(v, (0, lanes - heads))
    dtb, alog = padh(dt_bias), padh(a_log)
    yz = _ssd_scan(xbc.reshape(bsz, seq, conv_dim), z.reshape(bsz, seq, inner), dt, jnp.swapaxes(dt, 1, 2),
                   conv_w, conv_b.reshape(1, -1), dtb.reshape(1, -1), dtb.reshape(-1, 1),
                   alog.reshape(1, -1), alog.reshape(-1, 1),
                   jnp.repeat(d_skip, SSM_HEAD_DIM).reshape(1, -1), norm_g.reshape(1, -1))
    return _proj_ln(xt, yz.reshape(bsz * seq, inner), w_out.astype(BF16), g, b, alpha)


def kernel(x, ln_g, ln_b, ffn1_w_in, ffn1_w_out, ffn2_w_in, ffn2_w_out, attn_w_in, attn_kv_norm, attn_w_uk,
           attn_w_uv, attn_w_out, ssm_w_in, ssm_conv_w, ssm_conv_b, ssm_dt_bias, ssm_a_log, ssm_d,
           ssm_norm_g, ssm_w_out):
    bsz, seq, d = x.shape
    depth = ln_g.shape[0]
    alpha = (2.0 * depth) ** 0.25
    f1_in, f1_out = _ffn_weights(ffn1_w_in, ffn1_w_out)
    f2_in, f2_out = _ffn_weights(ffn2_w_in, ffn2_w_out)
    ln = lambda i, k: (ln_g[i, k].reshape(1, d), ln_b[i, k].reshape(1, d))
    xt = x.reshape(bsz * seq, d)
    for i in range(depth):
        xt = _ffn_ln(xt, f1_in[i], f1_out[i], *ln(i, 0), alpha)
        j = i // 2
        if i % 2 == 0:
            xt = _dsa_layer(xt, bsz, seq, attn_w_in[j], attn_kv_norm[j], attn_w_uk[j], attn_w_uv[j],
                            attn_w_out[j], *ln(i, 1), alpha)
        else:
            xt = _ssd_layer(xt, bsz, seq, ssm_w_in[j], ssm_conv_w[j], ssm_conv_b[j], ssm_dt_bias[j],
                            ssm_a_log[j], ssm_d[j], ssm_norm_g[j], ssm_w_out[j], *ln(i, 1), alpha)
        xt = _ffn_ln(xt, f2_in[i], f2_out[i], *ln(i, 2), alpha)
    return xt.reshape(bsz, seq, d)
```

```python
import functools
import math

import jax
import jax.numpy as jnp
from jax import lax
from jax.experimental import pallas as pl
from jax.experimental.pallas import tpu as pltpu

F32 = jnp.float32
BF16 = jnp.bfloat16
I32 = jnp.int32

ATTN_HEADS = 16
ATTN_HEAD_DIM = 64
KV_RANK = 256
IDX_HEADS = 8
IDX_HEAD_DIM = 64
TOPK_MAX = 256
SSM_HEAD_DIM = 64
SSM_GROUPS = 4
SSM_STATE = 128
SSM_CONV = 4
LN_EPS = 1e-5
RMS_EPS = 1e-6

V7X_LANES = 128
V7X_SUBLANES = 8
V7X_MXU_DIM = 256
V7X_VMEM_BYTES = 64 * 1024 * 1024

ROW_TILE = 512
FFN_ROW_TILE = 1024
Q_TILE = 128
IDX_KEY_TILE = 512
SCORE_WIDTHS = (8, 4, 2, 1)
ATTN_KEY_TILE = 2 * V7X_MXU_DIM
ATTN_WIDTHS = (8, 4, 2, 1)
COUNT_ROWS = 64
SEARCH_FIXED_BITS = 19
SEARCH_GROUP_BITS = 4
SSD_CHUNK = 128
MASKED_LOGIT = -1e30
SAFE_LOGIT_BOUND = 40.0
NORM_SLACK = 1.01
LOG2_E = math.log2(math.e)
INT_MIN = -(2 ** 31)
NEG_INF_KEY = INT_MIN + 0x007FFFFF


def _vmem_limit(nbytes):
    return int(min(nbytes, V7X_VMEM_BYTES - 8 * 1024 * 1024))


def _resident(shape):
    nd = len(shape)
    return pl.BlockSpec(shape, lambda *_: (0,) * nd, pipeline_mode=pl.Buffered(1))


def _layer_norm(y, g, b):
    mu = jnp.mean(y, axis=-1, keepdims=True)
    d = y - mu
    var = jnp.mean(d * d, axis=-1, keepdims=True)
    return d * lax.rsqrt(var + LN_EPS) * g + b


def _silu(v):
    return v * jax.nn.sigmoid(v)


def _ffn_ln_kernel(x_ref, win_ref, wout_ref, g_ref, b_ref, o_ref, acc_ref, *, n_chunk, alpha):
    x = x_ref[...]
    xb = x.astype(BF16)
    for c in range(n_chunk):
        gate = jnp.dot(xb, win_ref[c], preferred_element_type=F32)
        up = jnp.dot(xb, win_ref[n_chunk + c], preferred_element_type=F32)
        act = (_silu(gate) * up).astype(BF16)
        part = jnp.dot(act, wout_ref[c], preferred_element_type=F32)
        if c == 0:
            acc_ref[...] = part
        else:
            acc_ref[...] += part
    y = alpha * x + 0.5 * acc_ref[...]
    o_ref[...] = _layer_norm(y, g_ref[...], b_ref[...])


def _ffn_ln(x, win, wout, g, b, alpha):
    t, d = x.shape
    n2, _, ck = win.shape
    n_chunk = n2 // 2
    tm = min(FFN_ROW_TILE, t)
    est = (win.size + wout.size) * 2 + 5 * tm * d * 4 + 6 * tm * ck * 4 + (8 << 20)
    return pl.pallas_call(
        functools.partial(_ffn_ln_kernel, n_chunk=n_chunk, alpha=alpha),
        grid=(t // tm,),
        in_specs=[pl.BlockSpec((tm, d), lambda i: (i, 0)),
                  _resident(win.shape), _resident(wout.shape),
                  _resident(g.shape), _resident(b.shape)],
        out_specs=pl.BlockSpec((tm, d), lambda i: (i, 0)),
        out_shape=jax.ShapeDtypeStruct((t, d), F32),
        scratch_shapes=[pltpu.VMEM((tm, d), F32)],
        compiler_params=pltpu.CompilerParams(dimension_semantics=("arbitrary",),
                                             vmem_limit_bytes=_vmem_limit(est)),
        name="ffn_ln",
    )(x, win, wout, g, b)


def _proj_ln_kernel(x_ref, a_ref, w_ref, g_ref, b_ref, o_ref, *, alpha):
    m = jnp.dot(a_ref[...], w_ref[...], preferred_element_type=F32)
    o_ref[...] = _layer_norm(alpha * x_ref[...] + m, g_ref[...], b_ref[...])


def _proj_ln(x, a, w, g, b, alpha):
    t, d = x.shape
    k = a.shape[1]
    tm = min(ROW_TILE, t)
    est = w.size * 2 + 6 * tm * d * 4 + 2 * tm * k * 2 + (8 << 20)
    return pl.pallas_call(
        functools.partial(_proj_ln_kernel, alpha=alpha),
        grid=(t // tm,),
        in_specs=[pl.BlockSpec((tm, d), lambda i: (i, 0)),
                  pl.BlockSpec((tm, k), lambda i: (i, 0)),
                  _resident(w.shape), _resident(g.shape), _resident(b.shape)],
        out_specs=pl.BlockSpec((tm, d), lambda i: (i, 0)),
        out_shape=jax.ShapeDtypeStruct((t, d), F32),
        compiler_params=pltpu.CompilerParams(dimension_semantics=("arbitrary",),
                                             vmem_limit_bytes=_vmem_limit(est)),
        name="proj_ln",
    )(x, a, w, g, b)


def _attn_inproj_kernel(x_ref, wq_ref, wc_ref, wqi_ref, wkw_ref, kvg_ref,
                        q_ref, c_ref, qi_ref, ki_ref, wi_ref, *, w_scale):
    xb = x_ref[...].astype(BF16)
    q_ref[...] = jnp.dot(xb, wq_ref[...], preferred_element_type=F32).astype(BF16)
    c = jnp.dot(xb, wc_ref[...], preferred_element_type=F32)
    ms = jnp.mean(c * c, axis=-1, keepdims=True)
    c_ref[...] = (c * lax.rsqrt(ms + RMS_EPS) * kvg_ref[...]).astype(BF16)
    qi_ref[...] = jnp.dot(xb, wqi_ref[...], preferred_element_type=F32).astype(BF16)
    kw = jnp.dot(xb, wkw_ref[...], preferred_element_type=F32)
    ki_ref[...] = kw[:, :IDX_HEAD_DIM].astype(BF16)
    wi_ref[...] = kw[:, IDX_HEAD_DIM:IDX_HEAD_DIM + IDX_HEADS] * (IDX_HEADS ** -0.5) * w_scale


def _attn_inproj(x, wq, wc, wqi, wkw, kvg):
    t, d = x.shape
    tm = min(ROW_TILE, t)
    nq, nc, nqi = wq.shape[1], wc.shape[1], wqi.shape[1]
    est = (wq.size + wc.size + wqi.size + wkw.size) * 2 + 2 * tm * d * 4 + 8 * tm * (nq + nc + nqi) + (8 << 20)
    row = lambda n: pl.BlockSpec((tm, n), lambda i: (i, 0))
    return pl.pallas_call(
        functools.partial(_attn_inproj_kernel, w_scale=IDX_HEAD_DIM ** -0.5),
        grid=(t // tm,),
        in_specs=[row(d), _resident(wq.shape), _resident(wc.shape), _resident(wqi.shape),
                  _resident(wkw.shape), _resident(kvg.shape)],
        out_specs=[row(nq), row(nc), row(nqi), row(IDX_HEAD_DIM), row(IDX_HEADS)],
        out_shape=[jax.ShapeDtypeStruct((t, nq), BF16), jax.ShapeDtypeStruct((t, nc), BF16),
                   jax.ShapeDtypeStruct((t, nqi), BF16), jax.ShapeDtypeStruct((t, IDX_HEAD_DIM), BF16),
                   jax.ShapeDtypeStruct((t, IDX_HEADS), F32)],
        compiler_params=pltpu.CompilerParams(dimension_semantics=("arbitrary",),
                                             vmem_limit_bytes=_vmem_limit(est)),
        name="attn_inproj",
    )(x, wq, wc, wqi, wkw, kvg)


def _float_key(v):
    bits = pltpu.bitcast(v, I32)
    return bits ^ ((bits >> 31) & 0x7FFFFFFF)


def _store_keys(key_scr, off, score, causal):
    key_scr[pl.ds(off, score.shape[0]), :] = _float_key(jnp.where(causal, score, -jnp.inf))


def _topk_threshold(key_scr, thr_scr, lim_scr, cnt_scr, n_blk, n_causal, *, topk, seq):
    tki = IDX_KEY_TILE
    tq = key_scr.shape[1]
    kf = float(topk)

    def count(pred):
        def body(kb, c):
            off = pl.multiple_of(kb * tki, tki)
            keys = key_scr[pl.ds(off, tki), :]
            k_pos = off + lax.broadcasted_iota(I32, (tki, tq), 0)
            hit = jnp.where(pred(keys, k_pos), 1.0, 0.0)
            return c + jnp.sum(hit.reshape(tki // COUNT_ROWS, COUNT_ROWS, tq), axis=0)
        c = lax.fori_loop(0, n_blk, body, jnp.zeros((COUNT_ROWS, tq), F32))
        return jnp.sum(c, axis=0, keepdims=True)

    def add_bit(j, st):
        thr, c_thr = st
        cand = thr + jnp.left_shift(jnp.int32(1), 30 - j)
        c = count(lambda k, p: k >= cand)
        ok = c >= kf
        return jnp.where(ok, cand, thr), jnp.where(ok, c, c_thr)

    take_all = n_causal <= kf
    c0 = count(lambda k, p: k >= 0)
    n_all = jnp.full((1, tq), n_blk * tki).astype(F32)
    thr, c_thr = lax.fori_loop(
        0, SEARCH_FIXED_BITS, add_bit,
        (jnp.where(c0 >= kf, 0, INT_MIN).astype(I32), jnp.where(c0 >= kf, c0, n_all)))
    thr_scr[...] = thr
    cnt_scr[...] = c_thr
    for lo in range(SEARCH_FIXED_BITS, 31, SEARCH_GROUP_BITS):
        @pl.when(jnp.max(jnp.where((cnt_scr[...] != kf) & ~take_all, 1.0, 0.0)) > 0.5)
        def _(lo=lo):
            thr, c_thr = lax.fori_loop(lo, min(lo + SEARCH_GROUP_BITS, 31), add_bit, (thr_scr[...], cnt_scr[...]))
            thr_scr[...] = thr
            cnt_scr[...] = c_thr
    c_thr = cnt_scr[...]
    thr = jnp.maximum(thr_scr[...], NEG_INF_KEY + 1)
    tied = (c_thr > kf) & ~take_all
    thr_scr[...] = thr
    lim_scr[...] = jnp.full((1, tq), seq, I32)

    @pl.when(jnp.max(jnp.where(tied, 1.0, 0.0)) > 0.5)
    def _():
        n_bits = max(1, (seq - 1).bit_length())
        need = kf - count(lambda k, p: k > thr)

        def lim_body(j, lim):
            cand = lim + jnp.left_shift(jnp.int32(1), n_bits - 1 - j)
            return jnp.where(count(lambda k, p: (k == thr) & (p < cand)) < need, cand, lim)

        lim = lax.fori_loop(0, n_bits, lim_body, jnp.zeros((1, tq), I32))
        lim_scr[...] = jnp.where(tied, lim, seq)


def _dsa_kernel(x_ref, q_ref, qi_ref, wit_ref, ckv_ref, ckvt_ref, ki_ref, wukt_ref, wuvt_ref, wo_ref,
                g_ref, b_ref, cmax_ref, o_ref,
                key_scr, qabs_scr, qipair_scr, acc_scr, m_scr, l_scr, bound_scr, ot_scr, thr_scr, lim_scr, cnt_scr,
                *, topk, seq, alpha):
    i = pl.program_id(1)
    tq, h_n, dh, r_n = Q_TILE, ATTN_HEADS, ATTN_HEAD_DIM, KV_RANK
    hi_n, di = IDX_HEADS, IDX_HEAD_DIM
    tki, tk = IDX_KEY_TILE, ATTN_KEY_TILE
    pair = 2 * tq
    q0 = i * tq
    n_kv_blk = q0 // tk + 1
    n_idx_blk = pl.cdiv(n_kv_blk * tk, tki)

    for h in range(h_n):
        qa = lax.dot_general(wukt_ref[h], q_ref[0, :, h * dh:(h + 1) * dh], (((1,), (1,)), ((), ())),
                             preferred_element_type=F32)
        qb = (qa * (dh ** -0.5 * LOG2_E)).astype(BF16)
        qabs_scr[:, h * tq:(h + 1) * tq] = qb
        qf = qb.astype(F32)
        bound_scr[:, h * tq:(h + 1) * tq] = jnp.sqrt(jnp.sum(qf * qf, axis=0, keepdims=True)) * cmax_ref[...]
    for h in range(hi_n):
        qipair_scr[h // 2, (h % 2) * tq:(h % 2 + 1) * tq, :] = qi_ref[0, :, h * di:(h + 1) * di]

    wit = wit_ref[0]
    q_pos = q0 + lax.broadcasted_iota(I32, (1, tq), 1)

    def score_rows(off, rows):
        ki = ki_ref[0, pl.ds(off, rows), :]
        score = jnp.zeros((rows, tq), F32)
        for j in range(hi_n // 2):
            s = lax.dot_general(ki, qipair_scr[j], (((1,), (1,)), ((), ())),
                                preferred_element_type=F32)
            for e in range(2):
                h = 2 * j + e
                score = score + jnp.maximum(s[:, e * tq:(e + 1) * tq], 0.0) * wit[h:h + 1, :]
        k_pos = off + lax.broadcasted_iota(I32, (rows, tq), 0)
        _store_keys(key_scr, off, score, k_pos <= q_pos)

    done = 0
    for width in SCORE_WIDTHS:
        n = (n_idx_blk - done) // width

        def score_body(kb, carry, done=done, width=width):
            score_rows(pl.multiple_of((done + kb * width) * tki, tki), width * tki)
            return carry

        lax.fori_loop(0, n, score_body, 0)
        done = done + n * width
    _topk_threshold(key_scr, thr_scr, lim_scr, cnt_scr, n_idx_blk, (q_pos + 1).astype(F32), topk=topk, seq=seq)
    thr = thr_scr[...]
    lim = lim_scr[...]

    m_scr[...] = jnp.full(m_scr.shape, MASKED_LOGIT, F32)
    l_scr[...] = jnp.zeros(l_scr.shape, F32)
    acc_scr[...] = jnp.zeros(acc_scr.shape, F32)

    def attn_block(off, tk, running_max):
        keys = key_scr[pl.ds(off, tk), :]
        k_pos = off + lax.broadcasted_iota(I32, (tk, tq), 0)
        sel = (keys >= thr) & ((keys > thr) | (k_pos <= lim))
        bias = jnp.where(sel, 0.0, MASKED_LOGIT)
        bias = jnp.concatenate([bias, bias], axis=1)
        c_blk = ckv_ref[0, pl.ds(off, tk), :]
        ct_blk = ckvt_ref[0, :, pl.ds(off, tk)]
        for j in range(h_n // 2):
            cols = slice(j * pair, (j + 1) * pair)
            lg = jnp.dot(c_blk, qabs_scr[:, cols], preferred_element_type=F32) + bias
            if running_max:
                m_old = m_scr[:, cols]
                m_new = jnp.maximum(m_old, jnp.max(lg, axis=0, keepdims=True))
                a = jnp.exp2(m_old - m_new)
                p = jnp.exp2(lg - m_new)
                l_scr[:, cols] = a * l_scr[:, cols] + jnp.sum(p, axis=0, keepdims=True)
                m_scr[:, cols] = m_new
                pv = jnp.dot(ct_blk, p.astype(BF16), preferred_element_type=F32)
                acc_scr[:, cols] = acc_scr[:, cols] * a + pv
            else:
                p = jnp.exp2(lg - bound_scr[:, cols])
                l_scr[:, cols] += jnp.sum(p, axis=0, keepdims=True)
                acc_scr[:, cols] += jnp.dot(ct_blk, p.astype(BF16), preferred_element_type=F32)

    def attn_loop(done, width, running_max):
        n = (n_kv_blk - done) // width

        def body(kb, carry):
            attn_block(pl.multiple_of((done + kb * width) * tk, tk), width * tk, running_max)
            return carry

        lax.fori_loop(0, n, body, 0)
        return done + n * width

    bounded = jnp.max(bound_scr[...]) < SAFE_LOGIT_BOUND

    @pl.when(bounded)
    def _():
        done = 0
        for width in ATTN_WIDTHS:
            done = attn_loop(done, width, False)

    @pl.when(jnp.logical_not(bounded))
    def _():
        attn_loop(0, 1, True)

    for h in range(h_n):
        cols = slice(h * tq, (h + 1) * tq)
        o_lat = (acc_scr[:, cols] / l_scr[:, cols]).astype(BF16)
        ot_scr[h * dh:(h + 1) * dh, :] = jnp.dot(wuvt_ref[h], o_lat, preferred_element_type=F32)
    o = ot_scr[...].T.astype(BF16)
    mix = jnp.dot(o, wo_ref[...], preferred_element_type=F32)
    o_ref[0] = _layer_norm(alpha * x_ref[0] + mix, g_ref[...], b_ref[...])


def _dsa_attn(x, q, qi, wit, ckv, ckvt, ki, wukt, wuvt, wo, g, b, cmax, alpha):
    bsz, seq, d = x.shape
    tq = Q_TILE
    topk = min(TOPK_MAX, seq // 4)
    h_n, r_n = ATTN_HEADS, KV_RANK
    qblk = lambda n: pl.BlockSpec((1, tq, n), lambda bi, i: (bi, i, 0))
    qblk_t = lambda n: pl.BlockSpec((1, n, tq), lambda bi, i: (bi, 0, i))
    per_batch = lambda shp: pl.BlockSpec((1,) + shp, lambda bi, i: (bi, 0, 0), pipeline_mode=pl.Buffered(1))
    attn_rows = ATTN_WIDTHS[0] * ATTN_KEY_TILE
    score_rows = SCORE_WIDTHS[0] * IDX_KEY_TILE
    est = (2 * seq * r_n * 2 + seq * V7X_LANES * 2 + tq * seq * 4
           + (wukt.size + wuvt.size + wo.size) * 2
           + h_n * tq * r_n * 6 + (attn_rows + score_rows) * 2 * tq * 12
           + 2 * h_n * ATTN_HEAD_DIM * tq * (2 + 4) + 6 * tq * d * 4 + (8 << 20))
    return pl.pallas_call(
        functools.partial(_dsa_kernel, topk=topk, seq=seq, alpha=alpha),
        grid=(bsz, seq // tq),
        in_specs=[qblk(d), qblk(q.shape[2]), qblk(qi.shape[2]), qblk_t(wit.shape[1]),
                  per_batch((seq, r_n)), per_batch((r_n, seq)), per_batch((seq, IDX_HEAD_DIM)),
                  _resident(wukt.shape), _resident(wuvt.shape), _resident(wo.shape),
                  _resident(g.shape), _resident(b.shape), _resident(cmax.shape)],
        out_specs=qblk(d),
        out_shape=jax.ShapeDtypeStruct((bsz, seq, d), F32),
        scratch_shapes=[pltpu.VMEM((seq, tq), I32),
                        pltpu.VMEM((r_n, h_n * tq), BF16),
                        pltpu.VMEM((IDX_HEADS // 2, 2 * tq, IDX_HEAD_DIM), BF16),
                        pltpu.VMEM((r_n, h_n * tq), F32),
                        pltpu.VMEM((1, h_n * tq), F32),
                        pltpu.VMEM((1, h_n * tq), F32),
                        pltpu.VMEM((1, h_n * tq), F32),
                        pltpu.VMEM((h_n * ATTN_HEAD_DIM, tq), F32),
                        pltpu.VMEM((1, tq), I32),
                        pltpu.VMEM((1, tq), I32),
                        pltpu.VMEM((1, tq), F32)],
        compiler_params=pltpu.CompilerParams(dimension_semantics=("arbitrary", "arbitrary"),
                                             vmem_limit_bytes=_vmem_limit(est)),
        name="dsa_attn",
    )(x, q, qi, wit, ckv, ckvt, ki, wukt, wuvt, wo, g, b, cmax)


def _ssd_inproj_kernel(x_ref, wz_ref, wx_ref, wdt_ref, z_ref, xbc_ref, dt_ref):
    xb = x_ref[...].astype(BF16)
    z_ref[...] = jnp.dot(xb, wz_ref[...], preferred_element_type=F32)
    xbc_ref[...] = jnp.dot(xb, wx_ref[...], preferred_element_type=F32)
    dt_ref[...] = jnp.dot(xb, wdt_ref[...], preferred_element_type=F32)


def _ssd_inproj(x, wz, wx, wdt):
    t, d = x.shape
    tm = min(ROW_TILE, t)
    nz, nx, nd = wz.shape[1], wx.shape[1], wdt.shape[1]
    est = (wz.size + wx.size + wdt.size) * 2 + 2 * tm * d * 4 + 3 * tm * (nz + nx + nd) * 4 + (8 << 20)
    row = lambda n: pl.BlockSpec((tm, n), lambda i: (i, 0))
    return pl.pallas_call(
        _ssd_inproj_kernel,
        grid=(t // tm,),
        in_specs=[row(d), _resident(wz.shape), _resident(wx.shape), _resident(wdt.shape)],
        out_specs=[row(nz), row(nx), row(nd)],
        out_shape=[jax.ShapeDtypeStruct((t, nz), F32), jax.ShapeDtypeStruct((t, nx), F32),
                   jax.ShapeDtypeStruct((t, nd), F32)],
        compiler_params=pltpu.CompilerParams(dimension_semantics=("arbitrary",),
                                             vmem_limit_bytes=_vmem_limit(est)),
        name="ssd_inproj",
    )(x, wz, wx, wdt)


def _cumsum(a, axis):
    n = a.shape[axis]
    idx = lax.broadcasted_iota(I32, a.shape, axis)
    s = 1
    while s < n:
        a = a + jnp.where(idx >= s, pltpu.roll(a, s, axis), 0.0)
        s *= 2
    return a


def _softplus(v):
    return jnp.maximum(v, 0.0) + jnp.log1p(jnp.exp(-jnp.abs(v)))


def _ssd_scan_kernel(xbc_ref, z_ref, dt_ref, dtt_ref, cw_ref, cb_ref, dtb_ref, dtbt_ref, alog_ref, alogt_ref,
                     dsk_ref, ng_ref, o_ref,
                     win_scr, state_scr, xdte_scr, y_scr, cd_scr, *, inner):
    c_idx = pl.program_id(1)
    ln = SSD_CHUNK
    g_n, n_st, p_dim = SSM_GROUPS, SSM_STATE, SSM_HEAD_DIM
    gw = inner // g_n
    pairs_per_group = gw // V7X_LANES
    tail = V7X_SUBLANES

    @pl.when(c_idx == 0)
    def _():
        win_scr[0:tail, :] = jnp.zeros((tail, win_scr.shape[1]), F32)
        state_scr[...] = jnp.zeros(state_scr.shape, F32)

    xbc = xbc_ref[0]
    win_scr[tail:tail + ln, :] = xbc
    conv = cb_ref[...] + cw_ref[0:1, :] * win_scr[tail - SSM_CONV + 1:tail - SSM_CONV + 1 + ln, :]
    for k in range(1, SSM_CONV):
        lo = tail - SSM_CONV + 1 + k
        conv = conv + cw_ref[k:k + 1, :] * win_scr[lo:lo + ln, :]
    win_scr[0:tail, :] = xbc[ln - tail:ln, :]
    act = _silu(conv)
    xs = act[:, :inner]
    bm = act[:, inner:inner + g_n * n_st]
    cm = act[:, inner + g_n * n_st:]

    a_row = -jnp.exp(alog_ref[...])
    a_col = -jnp.exp(alogt_ref[...])
    dt = _softplus(dt_ref[0] + dtb_ref[...])
    dtt = _softplus(dtt_ref[0] + dtbt_ref[...])
    acs = _cumsum(dt * a_row, 0)
    acst = _cumsum(dtt * a_col, 1)
    e_acs = jnp.exp(acs)
    last = acs[ln - 1:ln, :]
    e_end = jnp.exp(last - acs)
    e_last = jnp.exp(last)

    tri = lax.broadcasted_iota(I32, (ln, ln), 0) >= lax.broadcasted_iota(I32, (ln, ln), 1)
    low = lax.broadcasted_iota(I32, (ln, V7X_LANES), 1) < p_dim
    low1 = low[0:1, :]

    def pair_cols(v, h0):
        return jnp.where(low, v[:, h0:h0 + 1], v[:, h0 + 1:h0 + 2])

    for g in range(g_n):
        bm_g = bm[:, g * n_st:(g + 1) * n_st]
        cm_g = cm[:, g * n_st:(g + 1) * n_st].astype(BF16)
        bmt_g = bm_g.T.astype(BF16)
        cb = jnp.dot(cm_g, bmt_g, preferred_element_type=F32)
        y_off = jnp.dot(cm_g, state_scr[g].astype(BF16), preferred_element_type=F32)
        for jp in range(pairs_per_group):
            j = g * pairs_per_group + jp
            h0 = 2 * j
            lanes = slice(j * V7X_LANES, (j + 1) * V7X_LANES)
            d0 = jnp.where(tri, jnp.exp(acs[:, h0:h0 + 1] - acst[h0:h0 + 1, :]), 0.0)
            d1 = jnp.where(tri, jnp.exp(acs[:, h0 + 1:h0 + 2] - acst[h0 + 1:h0 + 2, :]), 0.0)
            m_pair = jnp.concatenate([(cb * d0).astype(BF16), (cb * d1).astype(BF16)], axis=1)
            x_pair = xs[:, lanes]
            xdt = x_pair * pair_cols(dt, h0)
            xb = xdt.astype(BF16)
            zero = jnp.zeros_like(xb)
            rhs = jnp.concatenate([jnp.where(low, xb, zero), jnp.where(low, zero, xb)], axis=0)
            y_diag = jnp.dot(m_pair, rhs, preferred_element_type=F32)
            y = y_diag + y_off[:, jp * V7X_LANES:(jp + 1) * V7X_LANES] * pair_cols(e_acs, h0)
            y_scr[:, lanes] = y + x_pair * dsk_ref[:, lanes]
            xdte_scr[:, lanes] = (xdt * pair_cols(e_end, h0)).astype(BF16)
            cd_scr[:, lanes] = jnp.where(low1, e_last[:, h0:h0 + 1], e_last[:, h0 + 1:h0 + 2])
        new = jnp.dot(bmt_g, xdte_scr[:, g * gw:(g + 1) * gw], preferred_element_type=F32)
        state_scr[g] = state_scr[g] * cd_scr[:, g * gw:(g + 1) * gw] + new

    yz = y_scr[...] * _silu(z_ref[0])
    for g in range(g_n):
        blk = yz[:, g * gw:(g + 1) * gw]
        ms = jnp.mean(blk * blk, axis=-1, keepdims=True)
        o_ref[0, :, g * gw:(g + 1) * gw] = (blk * lax.rsqrt(ms + RMS_EPS)
                                            * ng_ref[:, g * gw:(g + 1) * gw]).astype(BF16)


def _ssd_scan(xbc, z, dt, dtt, cw, cb, dtb, dtbt, alog, alogt, dsk, ng):
    bsz, seq, conv_dim = xbc.shape
    inner = z.shape[2]
    ln = SSD_CHUNK
    lanes = dt.shape[2]
    gw = inner // SSM_GROUPS
    blk = lambda n: pl.BlockSpec((1, ln, n), lambda bi, c: (bi, c, 0))
    est = (2 * ln * (conv_dim + inner) * 4 + 2 * ln * inner * 2 + (ln + 8) * conv_dim * 4
           + SSM_GROUPS * SSM_STATE * gw * 4 + ln * inner * 6 + 12 * ln * conv_dim * 4 + (8 << 20))
    return pl.pallas_call(
        functools.partial(_ssd_scan_kernel, inner=inner),
        grid=(bsz, seq // ln),
        in_specs=[blk(conv_dim), blk(inner), blk(lanes),
                  pl.BlockSpec((1, lanes, ln), lambda bi, c: (bi, 0, c)),
                  _resident(cw.shape), _resident(cb.shape), _resident(dtb.shape), _resident(dtbt.shape),
                  _resident(alog.shape), _resident(alogt.shape), _resident(dsk.shape), _resident(ng.shape)],
        out_specs=blk(inner),
        out_shape=jax.ShapeDtypeStruct((bsz, seq, inner), BF16),
        scratch_shapes=[pltpu.VMEM((ln + V7X_SUBLANES, conv_dim), F32),
                        pltpu.VMEM((SSM_GROUPS, SSM_STATE, gw), F32),
                        pltpu.VMEM((ln, inner), BF16),
                        pltpu.VMEM((ln, inner), F32),
                        pltpu.VMEM((1, inner), F32)],
        compiler_params=pltpu.CompilerParams(dimension_semantics=("arbitrary", "arbitrary"),
                                             vmem_limit_bytes=_vmem_limit(est)),
        name="ssd_scan",
    )(xbc, z, dt, dtt, cw, cb, dtb, dtbt, alog, alogt, dsk, ng)


def _ffn_weights(w_in, w_out):
    n_l, d, f2 = w_in.shape
    f = f2 // 2
    ck = V7X_MXU_DIM if f % V7X_MXU_DIM == 0 else V7X_LANES
    n = f // ck
    win = w_in.astype(BF16).reshape(n_l, d, 2, n, ck).transpose(0, 2, 3, 1, 4).reshape(n_l, 2 * n, d, ck)
    wout = w_out.astype(BF16).reshape(n_l, n, ck, d)
    return win, wout


def _dsa_layer(xt, bsz, seq, w_in, kvg, w_uk, w_uv, w_out, g, b, alpha):
    d = xt.shape[1]
    o1 = ATTN_HEADS * ATTN_HEAD_DIM
    o2 = o1 + KV_RANK
    o3 = o2 + IDX_HEADS * IDX_HEAD_DIM
    wb = w_in.astype(BF16)
    pad = V7X_LANES - IDX_HEAD_DIM - IDX_HEADS
    wkw = jnp.pad(wb[:, o3:], ((0, 0), (0, pad)))
    q, ckv, qi, ki, wi = _attn_inproj(xt, wb[:, :o1], wb[:, o1:o2], wb[:, o2:o3], wkw, kvg.reshape(1, -1))
    cmax = (jnp.max(jnp.abs(kvg)) * (KV_RANK ** 0.5 * NORM_SLACK)).reshape(1, 1)
    per_seq = lambda v: v.reshape(bsz, seq, -1)
    per_seq_t = lambda v: jnp.swapaxes(per_seq(v), 1, 2)
    out = _dsa_attn(per_seq(xt), per_seq(q), per_seq(qi), per_seq_t(wi), per_seq(ckv), per_seq_t(ckv),
                    per_seq(ki), jnp.swapaxes(w_uk, 1, 2).astype(BF16), jnp.swapaxes(w_uv, 1, 2).astype(BF16),
                    w_out.astype(BF16), g, b, cmax, alpha)
    return out.reshape(bsz * seq, d)


def _ssd_layer(xt, bsz, seq, w_in, conv_w, conv_b, dt_bias, a_log, d_skip, norm_g, w_out, g, b, alpha):
    inner = norm_g.shape[0]
    heads = dt_bias.shape[0]
    conv_dim = conv_w.shape[1]
    lanes = V7X_LANES * pl.cdiv(heads, V7X_LANES)
    wb = w_in.astype(BF16)
    wdt = jnp.pad(wb[:, inner + conv_dim:], ((0, 0), (0, lanes - heads)))
    z, xbc, dt = _ssd_inproj(xt, wb[:, :inner], wb[:, inner:inner + conv_dim], wdt)
    dt = dt.reshape(bsz, seq, lanes)
    padh = lambda v: jnp.pad(v, (0, lanes - heads))
    dtb, alog = padh(dt_bias), padh(a_log)
    yz = _ssd_scan(xbc.reshape(bsz, seq, conv_dim), z.reshape(bsz, seq, inner), dt, jnp.swapaxes(dt, 1, 2),
                   conv_w, conv_b.reshape(1, -1), dtb.reshape(1, -1), dtb.reshape(-1, 1),
                   alog.reshape(1, -1), alog.reshape(-1, 1),
                   jnp.repeat(d_skip, SSM_HEAD_DIM).reshape(1, -1), norm_g.reshape(1, -1))
    return _proj_ln(xt, yz.reshape(bsz * seq, inner), w_out.astype(BF16), g, b, alpha)


def kernel(x, ln_g, ln_b, ffn1_w_in, ffn1_w_out, ffn2_w_in, ffn2_w_out, attn_w_in, attn_kv_norm, attn_w_uk,
           attn_w_uv, attn_w_out, ssm_w_in, ssm_conv_w, ssm_conv_b, ssm_dt_bias, ssm_a_log, ssm_d,
           ssm_norm_g, ssm_w_out):
    bsz, seq, d = x.shape
    depth = ln_g.shape[0]
    alpha = (2.0 * depth) ** 0.25
    f1_in, f1_out = _ffn_weights(ffn1_w_in, ffn1_w_out)
    f2_in, f2_out = _ffn_weights(ffn2_w_in, ffn2_w_out)
    ln = lambda i, k: (ln_g[i, k].reshape(1, d), ln_b[i, k].reshape(1, d))
    xt = x.reshape(bsz * seq, d)
    for i in range(depth):
        xt = _ffn_ln(xt, f1_in[i], f1_out[i], *ln(i, 0), alpha)
        j = i // 2
        if i % 2 == 0:
            xt = _dsa_layer(xt, bsz, seq, attn_w_in[j], attn_kv_norm[j], attn_w_uk[j], attn_w_uv[j],
                            attn_w_out[j], *ln(i, 1), alpha)
        else:
            xt = _ssd_layer(xt, bsz, seq, ssm_w_in[j], ssm_conv_w[j], ssm_conv_b[j], ssm_dt_bias[j],
                            ssm_a_log[j], ssm_d[j], ssm_norm_g[j], ssm_w_out[j], *ln(i, 1), alpha)
        xt = _ffn_ln(xt, f2_in[i], f2_out[i], *ln(i, 2), alpha)
    return xt.reshape(bsz, seq, d)
```
